```python
import functools
import jax, jax.numpy as jnp
from jax import lax
import numpy as np

D_MODEL = 2048
BATCH = 8
SEQ = 4096
DEPTH = 2
DEC_BATCH = 8
DEC_SEQ = 16
PAST_LEN = 1024

CHUNK = 64
N_MIXERS = 2
N_RET_LAYERS = (DEPTH + 1) // 2
N_MLSTM_LAYERS = DEPTH // 2
RET_HEADS = 8
RET_QK_DIM = D_MODEL
RET_V_DIM = 2 * D_MODEL
RET_DK = RET_QK_DIM // RET_HEADS
RET_DV = RET_V_DIM // RET_HEADS
RET_IN = 2 * RET_QK_DIM + 2 * RET_V_DIM
ROPE_BASE = 10000.0
MLSTM_INNER = 2 * D_MODEL
MLSTM_HEADS = 4
MLSTM_DH = MLSTM_INNER // MLSTM_HEADS
CONV_WIDTH = 4
QKV_BLOCK = 4
N_QKV_BLOCKS = MLSTM_INNER // QKV_BLOCK
EPS = 1e-6
M_INIT = -1e30

kernel_name = "retnet_mlstm_interleaved_stream_step"


def rmsnorm(x, w):
    xf = x.astype(jnp.float32)
    y = xf * lax.rsqrt(jnp.mean(xf * xf, axis=-1, keepdims=True) + EPS)
    return (y * w.astype(jnp.float32)).astype(x.dtype)


def head_rmsnorm(h, w):
    h = h * lax.rsqrt(jnp.mean(h * h, axis=-1, keepdims=True) + EPS)
    return h.reshape(h.shape[0], h.shape[1], -1) * w.astype(jnp.float32)


def head_layernorm(h, w):
    h = h - jnp.mean(h, axis=-1, keepdims=True)
    h = h * lax.rsqrt(jnp.mean(h * h, axis=-1, keepdims=True) + EPS)
    return h.reshape(h.shape[0], h.shape[1], -1) * w.astype(jnp.float32)


def rotary(x, pos):
    half = x.shape[-1] // 2
    inv_freq = ROPE_BASE ** (-jnp.linspace(0.0, 1.0, half, dtype=jnp.float32))
    ang = pos.astype(jnp.float32)[:, None] * inv_freq[None, :]
    cos = jnp.cos(ang)[None, :, None, :]
    sin = jnp.sin(ang)[None, :, None, :]
    x1, x2 = x[..., :half], x[..., half:]
    return jnp.concatenate([x1 * cos - x2 * sin, x1 * sin + x2 * cos], axis=-1)


def scan_chunks(step, carry, xs):
    T = xs[0].shape[1]
    if T <= CHUNK:
        return step(carry, xs)
    n = T // CHUNK

    def split(a):
        return jnp.moveaxis(a.reshape(a.shape[0], n, CHUNK, *a.shape[2:]), 1, 0)

    carry, ys = lax.scan(step, carry, tuple(split(a) for a in xs))
    ys = jnp.moveaxis(ys, 0, 1)
    return carry, ys.reshape(ys.shape[0], T, *ys.shape[3:])


def retention_step(log_gamma, s, inp):
    q, k, v = inp
    L = q.shape[1]
    idx = jnp.arange(L, dtype=jnp.float32)
    rel = idx[:, None] - idx[None, :]
    decay = jnp.where((rel >= 0)[None], jnp.exp(log_gamma[:, None, None] * jnp.maximum(rel, 0.0)[None]), 0.0)
    scores = jnp.einsum("bihd,bjhd->bhij", q, k) * decay[None]
    intra = jnp.einsum("bhij,bjhv->bihv", scores, v)
    cross = jnp.einsum("bihd,bhdv->bihv", q, s) * jnp.exp(log_gamma[None, :] * (idx[:, None] + 1.0))[None, :, :, None]
    k_tail = k * jnp.exp(log_gamma[None, :] * (L - 1.0 - idx[:, None]))[None, :, :, None]
    s_new = jnp.exp(log_gamma * L)[None, :, None, None] * s + jnp.einsum("bjhd,bjhv->bhdv", k_tail, v)
    return s_new, intra + cross


def retention_mixer(h, pos, s0, w_in, gn_w, w_out):
    B, T, _ = h.shape
    q, k, v, g = jnp.split(h @ w_in, [RET_QK_DIM, 2 * RET_QK_DIM, 2 * RET_QK_DIM + RET_V_DIM], axis=-1)
    q = rotary(q.reshape(B, T, RET_HEADS, RET_DK).astype(jnp.float32), pos)
    k = rotary(k.reshape(B, T, RET_HEADS, RET_DK).astype(jnp.float32), pos) * (RET_DK ** -0.5)
    v = v.reshape(B, T, RET_HEADS, RET_DV).astype(jnp.float32)
    log_gamma = jnp.log1p(-jnp.exp2(-5.0 - jnp.arange(RET_HEADS, dtype=jnp.float32)))
    s_new, o = scan_chunks(functools.partial(retention_step, log_gamma), s0.astype(jnp.float32), (q, k, v))
    o = head_rmsnorm(o, gn_w).astype(h.dtype)
    return (jax.nn.silu(g) * o) @ w_out, s_new


def mlstm_step(carry, inp):
    c_prev, n_prev, m_prev = carry
    q, k, v, ig, lf = inp
    L = q.shape[1]
    b = jnp.moveaxis(jnp.cumsum(lf, axis=1), 1, 2)
    ig_h = jnp.moveaxis(ig, 1, 2)
    causal = jnp.tril(jnp.ones((L, L), dtype=bool))
    log_d = jnp.where(causal, b[..., :, None] - b[..., None, :] + ig_h[..., None, :], -jnp.inf)
    log_past = b + m_prev[..., None]
    m = jnp.maximum(log_past, jnp.max(log_d, axis=-1))
    d = jnp.exp(log_d - m[..., None])
    w_past = jnp.exp(log_past - m)
    s = jnp.einsum("bihd,bjhd->bhij", q, k) * d
    num = jnp.einsum("bhij,bjhv->bihv", s, v) + jnp.einsum("bihd,bhdv->bihv", q, c_prev) * jnp.moveaxis(w_past, 2, 1)[..., None]
    den = jnp.sum(s, axis=-1) + jnp.einsum("bihd,bhd->bhi", q, n_prev) * w_past
    den = jnp.maximum(jnp.abs(den), jnp.exp(-m))
    h = num / jnp.moveaxis(den, 2, 1)[..., None]
    m_new = m[..., -1]
    w_tail = jnp.exp(b[..., -1:] - b + ig_h - m_new[..., None])
    w_carry = jnp.exp(b[..., -1] + m_prev - m_new)
    k_tail = k * jnp.moveaxis(w_tail, 2, 1)[..., None]
    c_new = w_carry[..., None, None] * c_prev + jnp.einsum("bjhd,bjhv->bhdv", k_tail, v)
    n_new = w_carry[..., None] * n_prev + jnp.sum(k_tail, axis=1)
    return (c_new, n_new, m_new), h


def headwise(x, w):
    B, T, _ = x.shape
    return jnp.einsum("btnc,ncd->btnd", x.reshape(B, T, N_QKV_BLOCKS, QKV_BLOCK), w).reshape(B, T, MLSTM_INNER)


def mlstm_mixer(h, conv_buf, c0, n0, m0, w_in, conv_w, conv_b, w_q, w_k, w_v,
                w_ig, b_ig, w_fg, b_fg, skip, gn_w, w_out):
    B, T, _ = h.shape
    x_m, z = jnp.split(h @ w_in, 2, axis=-1)
    xpad = jnp.concatenate([conv_buf.astype(x_m.dtype), x_m], axis=1)
    conv = conv_b + sum(xpad[:, w:w + T] * conv_w[w] for w in range(CONV_WIDTH))
    x_c = jax.nn.silu(conv)
    q = headwise(x_c, w_q)
    k = headwise(x_c, w_k)
    v = headwise(x_m, w_v)

    def gate(w, bias):
        return (q @ w[:MLSTM_INNER] + k @ w[MLSTM_INNER:2 * MLSTM_INNER]
                + v @ w[2 * MLSTM_INNER:] + bias).astype(jnp.float32)

    ig = gate(w_ig, b_ig)
    lf = jax.nn.log_sigmoid(gate(w_fg, b_fg))
    shp = (B, T, MLSTM_HEADS, MLSTM_DH)
    qh = q.reshape(shp).astype(jnp.float32)
    kh = k.reshape(shp).astype(jnp.float32) * (MLSTM_DH ** -0.5)
    vh = v.reshape(shp).astype(jnp.float32)
    (c, n, m), o = scan_chunks(mlstm_step, (c0.astype(jnp.float32), n0.astype(jnp.float32), m0.astype(jnp.float32)),
                               (qh, kh, vh, ig, lf))
    o = head_layernorm(o, gn_w).astype(h.dtype) + skip * x_c
    y = (o * jax.nn.silu(z)) @ w_out
    return y, c, n, m, xpad[:, -(CONV_WIDTH - 1):]


def setup_inputs(seed: int = 0) -> dict:
    key = jax.random.key(seed)
    ks = iter(jax.random.split(key, 40))

    def nrm(shape, scale):
        return jax.random.normal(next(ks), shape, jnp.float32) * scale

    NR, NM = N_RET_LAYERS, N_MLSTM_LAYERS
    return {
        "x_prompt": nrm((BATCH, SEQ, D_MODEL), 1.0),
        "x_sample": nrm((DEC_BATCH, DEC_SEQ, D_MODEL), 1.0),
        "state_ret": nrm((NR, DEC_BATCH, RET_HEADS, RET_DK, RET_DV), 0.05),
        "state_mlstm_c": nrm((NM, DEC_BATCH, MLSTM_HEADS, MLSTM_DH, MLSTM_DH), 0.02),
        "state_mlstm_n": nrm((NM, DEC_BATCH, MLSTM_HEADS, MLSTM_DH), 0.05),
        "state_mlstm_m": nrm((NM, DEC_BATCH, MLSTM_HEADS), 0.5),
        "state_mlstm_conv": nrm((NM, DEC_BATCH, CONV_WIDTH - 1, MLSTM_INNER), 1.0),
        "norm_w": 1.0 + nrm((DEPTH, D_MODEL), 0.02),
        "ret_w_in": nrm((NR, D_MODEL, RET_IN), D_MODEL ** -0.5),
        "ret_gn_w": 1.0 + nrm((NR, RET_V_DIM), 0.02),
        "ret_w_out": nrm((NR, RET_V_DIM, D_MODEL), RET_V_DIM ** -0.5),
        "ml_w_in": nrm((NM, D_MODEL, 2 * MLSTM_INNER), D_MODEL ** -0.5),
        "ml_conv_w": nrm((NM, CONV_WIDTH, MLSTM_INNER), CONV_WIDTH ** -0.5),
        "ml_conv_b": nrm((NM, MLSTM_INNER), 0.02),
        "ml_w_q": nrm((NM, N_QKV_BLOCKS, QKV_BLOCK, QKV_BLOCK), QKV_BLOCK ** -0.5),
        "ml_w_k": nrm((NM, N_QKV_BLOCKS, QKV_BLOCK, QKV_BLOCK), QKV_BLOCK ** -0.5),
        "ml_w_v": nrm((NM, N_QKV_BLOCKS, QKV_BLOCK, QKV_BLOCK), QKV_BLOCK ** -0.5),
        "ml_w_ig": nrm((NM, 3 * MLSTM_INNER, MLSTM_HEADS), (3 * MLSTM_INNER) ** -0.5),
        "ml_b_ig": nrm((NM, MLSTM_HEADS), 0.1),
        "ml_w_fg": nrm((NM, 3 * MLSTM_INNER, MLSTM_HEADS), (3 * MLSTM_INNER) ** -0.5),
        "ml_b_fg": jnp.linspace(3.0, 6.0, MLSTM_HEADS, dtype=jnp.float32)[None, :] + nrm((NM, MLSTM_HEADS), 0.1),
        "ml_skip": 1.0 + nrm((NM, MLSTM_INNER), 0.02),
        "ml_gn_w": 1.0 + nrm((NM, MLSTM_INNER), 0.02),
        "ml_w_out": nrm((NM, MLSTM_INNER, D_MODEL), MLSTM_INNER ** -0.5),
        "norm_f": 1.0 + nrm((D_MODEL,), 0.02),
    }


def reference(x_prompt, x_sample, state_ret, state_mlstm_c, state_mlstm_n, state_mlstm_m, state_mlstm_conv,
              norm_w, ret_w_in, ret_gn_w, ret_w_out, ml_w_in, ml_conv_w, ml_conv_b, ml_w_q, ml_w_k, ml_w_v,
              ml_w_ig, ml_b_ig, ml_w_fg, ml_b_fg, ml_skip, ml_gn_w, ml_w_out, norm_f):
    Bp, Tp, _ = x_prompt.shape
    Bs, Ts, _ = x_sample.shape
    pos_p = jnp.arange(Tp, dtype=jnp.int32)
    pos_s = PAST_LEN + jnp.arange(Ts, dtype=jnp.int32)
    hp, hs = x_prompt, x_sample
    ret_p, ret_s = [], []
    mc_p, mn_p, mm_p, cv_p = [], [], [], []
    mc_s, mn_s, mm_s, cv_s = [], [], [], []
    for layer in range(DEPTH):
        j = layer // N_MIXERS
        xp = rmsnorm(hp, norm_w[layer])
        xs = rmsnorm(hs, norm_w[layer])
        if layer % N_MIXERS == 0:
            s0p = jnp.zeros((Bp, RET_HEADS, RET_DK, RET_DV), jnp.float32)
            op, sp = retention_mixer(xp, pos_p, s0p, ret_w_in[j], ret_gn_w[j], ret_w_out[j])
            os_, ss = retention_mixer(xs, pos_s, state_ret[j], ret_w_in[j], ret_gn_w[j], ret_w_out[j])
            ret_p.append(sp)
            ret_s.append(ss)
        else:
            wts = (ml_w_in[j], ml_conv_w[j], ml_conv_b[j], ml_w_q[j], ml_w_k[j], ml_w_v[j],
                   ml_w_ig[j], ml_b_ig[j], ml_w_fg[j], ml_b_fg[j], ml_skip[j], ml_gn_w[j], ml_w_out[j])
            op, cp, np_, mp, bp = mlstm_mixer(
                xp, jnp.zeros((Bp, CONV_WIDTH - 1, MLSTM_INNER), xp.dtype),
                jnp.zeros((Bp, MLSTM_HEADS, MLSTM_DH, MLSTM_DH), jnp.float32),
                jnp.zeros((Bp, MLSTM_HEADS, MLSTM_DH), jnp.float32),
                jnp.full((Bp, MLSTM_HEADS), M_INIT, jnp.float32), *wts)
            os_, cs, ns, ms, bs = mlstm_mixer(
                xs, state_mlstm_conv[j], state_mlstm_c[j], state_mlstm_n[j], state_mlstm_m[j], *wts)
            mc_p.append(cp); mn_p.append(np_); mm_p.append(mp); cv_p.append(bp)
            mc_s.append(cs); mn_s.append(ns); mm_s.append(ms); cv_s.append(bs)
        hp = hp + op
        hs = hs + os_
    y_prompt = rmsnorm(hp, norm_f)
    y_sample = rmsnorm(hs, norm_f)
    return (y_prompt, y_sample,
            jnp.stack(ret_p), jnp.stack(mc_p), jnp.stack(mn_p), jnp.stack(mm_p), jnp.stack(cv_p),
            jnp.stack(ret_s), jnp.stack(mc_s), jnp.stack(mn_s), jnp.stack(mm_s), jnp.stack(cv_s))
```

```python
import functools

import jax
import jax.numpy as jnp
from jax import lax
from jax.experimental import pallas as pl
from jax.experimental.pallas import tpu as pltpu

F32 = jnp.float32
BF16 = jnp.bfloat16

RET_HEADS = 8
MLSTM_HEADS = 4
CONV_WIDTH = 4
QKV_BLOCK = 4
ROPE_BASE = 10000.0
EPS = 1e-6
M_INIT = -1e30
PAST_LEN = 1024

LANES = 128
MXU_DIM = 256
BF16_SUBLANES = 16
GATE_LANES = 128
VMEM_LIMIT_BYTES = 56 * 1024 * 1024

PROMPT_CHUNK = 256


def _params(*sem):
    return pltpu.CompilerParams(dimension_semantics=sem, vmem_limit_bytes=VMEM_LIMIT_BYTES)


def _pick(n, prefs):
    for p in prefs:
        if n % p == 0:
            return p
    return n


def _silu(x):
    return x * (1.0 / (1.0 + jnp.exp(-x)))


def _norm_proj_body(*refs, rot_tiles, k_tiles, k_scale, head_dim):
    if rot_tiles:
        x_ref, nw_ref, w_ref, cos_ref, sin_ref, o_ref, xn_ref = refs
    else:
        x_ref, nw_ref, w_ref, o_ref, xn_ref = refs
    j = pl.program_id(1)

    @pl.when(j == 0)
    def _():
        x = x_ref[...]
        ms = jnp.mean(x * x, axis=-1, keepdims=True)
        xn_ref[...] = (x * lax.rsqrt(ms + EPS) * nw_ref[...]).astype(BF16)

    acc = jnp.dot(xn_ref[...], w_ref[...], preferred_element_type=F32)
    if not rot_tiles:
        o_ref[...] = acc.astype(o_ref.dtype)
        return

    @pl.when(j < rot_tiles)
    def _():
        cos = cos_ref[...]
        sin = sin_ref[...]
        scale = jnp.where(j >= rot_tiles - k_tiles, k_scale, 1.0).astype(F32)
        half = head_dim // 2
        for hh in range(acc.shape[1] // head_dim):
            lo = hh * head_dim
            x1 = acc[:, lo:lo + half]
            x2 = acc[:, lo + half:lo + head_dim]
            o_ref[:, lo:lo + half] = ((x1 * cos - x2 * sin) * scale).astype(o_ref.dtype)
            o_ref[:, lo + half:lo + head_dim] = ((x1 * sin + x2 * cos) * scale).astype(o_ref.dtype)

    @pl.when(j >= rot_tiles)
    def _():
        o_ref[...] = acc.astype(o_ref.dtype)


def _norm_proj(x, nw, w, *, tm, tn, rot=None):
    M, D = x.shape
    N = w.shape[1]
    grid = (M // tm, N // tn)
    in_specs = [
        pl.BlockSpec((tm, D), lambda i, j: (i, 0)),
        pl.BlockSpec((1, D), lambda i, j: (0, 0)),
        pl.BlockSpec((D, tn), lambda i, j: (0, j)),
    ]
    args = [x, nw.reshape(1, D), w]
    kw = dict(rot_tiles=0, k_tiles=0, k_scale=1.0, head_dim=0)
    if rot is not None:
        cos, sin, qk_cols, k_cols, k_scale, head_dim = rot
        period = cos.shape[0] // tm
        half = head_dim // 2
        in_specs += [pl.BlockSpec((tm, half), lambda i, j: (i % period, 0))] * 2
        args += [cos, sin]
        kw = dict(rot_tiles=qk_cols // tn, k_tiles=k_cols // tn, k_scale=k_scale, head_dim=head_dim)
    return pl.pallas_call(
        functools.partial(_norm_proj_body, **kw),
        grid=grid,
        in_specs=in_specs,
        out_specs=pl.BlockSpec((tm, tn), lambda i, j: (i, j)),
        out_shape=jax.ShapeDtypeStruct((M, N), BF16),
        scratch_shapes=[pltpu.VMEM((tm, D), BF16)],
        compiler_params=_params("parallel", "arbitrary"),
        name="norm_proj_rot" if rot is not None else "norm_proj",
    )(*args)


def _out_proj_body(*refs, final_norm):
    if final_norm:
        u_ref, w_ref, h_ref, nf_ref, o_ref = refs
    else:
        u_ref, w_ref, h_ref, o_ref = refs
    y = h_ref[...] + jnp.dot(u_ref[...], w_ref[...], preferred_element_type=F32)
    if final_norm:
        ms = jnp.mean(y * y, axis=-1, keepdims=True)
        y = y * lax.rsqrt(ms + EPS) * nf_ref[...]
    o_ref[...] = y


def _out_proj(u, w, h, nf=None, *, tm):
    M, K = u.shape
    N = w.shape[1]
    in_specs = [
        pl.BlockSpec((tm, K), lambda i: (i, 0)),
        pl.BlockSpec((K, N), lambda i: (0, 0)),
        pl.BlockSpec((tm, N), lambda i: (i, 0)),
    ]
    args = [u, w, h]
    if nf is not None:
        in_specs.append(pl.BlockSpec((1, N), lambda i: (0, 0)))
        args.append(nf.reshape(1, N))
    return pl.pallas_call(
        functools.partial(_out_proj_body, final_norm=nf is not None),
        grid=(M // tm,),
        in_specs=in_specs,
        out_specs=pl.BlockSpec((tm, N), lambda i: (i, 0)),
        out_shape=jax.ShapeDtypeStruct((M, N), F32),
        compiler_params=_params("parallel"),
        name="out_proj_norm" if nf is not None else "out_proj",
    )(*args)


def _retention_body(*refs, L, zero_init):
    if zero_init:
        lg_ref, q_ref, k_ref, v_ref, g_ref, gnw_ref, u_ref, s_ref = refs
    else:
        lg_ref, q_ref, k_ref, v_ref, g_ref, gnw_ref, s0_ref, u_ref, s_ref = refs
    h = pl.program_id(1)
    c = pl.program_id(2)

    @pl.when(c == 0)
    def _():
        if zero_init:
            s_ref[...] = jnp.zeros_like(s_ref)
        else:
            s_ref[...] = s0_ref[...]

    lg = jnp.full((1, 1), lg_ref[h], F32)
    q = q_ref[...]
    k = k_ref[...]
    v = v_ref[...]
    ii = lax.broadcasted_iota(jnp.int32, (L, L), 0)
    jj = lax.broadcasted_iota(jnp.int32, (L, L), 1)
    rel = jnp.maximum(ii - jj, 0).astype(F32)
    decay = jnp.where(ii >= jj, jnp.exp(lg * rel), 0.0)
    scores = lax.dot_general(q, k, (((1,), (1,)), ((), ())), preferred_element_type=F32) * decay
    intra = jnp.dot(scores.astype(BF16), v, preferred_element_type=F32)
    s = s_ref[...]
    idx = lax.broadcasted_iota(jnp.int32, (L, 1), 0).astype(F32)
    cross = jnp.dot(q, s.astype(BF16), preferred_element_type=F32) * jnp.exp(lg * (idx + 1.0))
    o = intra + cross
    k_tail = (k.astype(F32) * jnp.exp(lg * (L - 1.0 - idx))).astype(BF16)
    s_ref[...] = jnp.exp(lg * float(L)) * s + lax.dot_general(
        k_tail, v, (((0,), (0,)), ((), ())), preferred_element_type=F32)
    o = o * lax.rsqrt(jnp.mean(o * o, axis=-1, keepdims=True) + EPS) * gnw_ref[...]
    u_ref[...] = (_silu(g_ref[...].astype(F32)) * o).astype(BF16)


def _retention(qkvg, log_gamma, gn_w, s0, *, B, T, L, dk, dv):
    H = RET_HEADS
    nC = T // L
    rv = dv // dk
    in_specs = [
        pl.BlockSpec(memory_space=pltpu.SMEM),
        pl.BlockSpec((L, dk), lambda b, h, c: (b * nC + c, h)),
        pl.BlockSpec((L, dk), lambda b, h, c: (b * nC + c, H + h)),
        pl.BlockSpec((L, dv), lambda b, h, c: (b * nC + c, 2 * H // rv + h)),
        pl.BlockSpec((L, dv), lambda b, h, c: (b * nC + c, 2 * H // rv + H + h)),
        pl.BlockSpec((1, dv), lambda b, h, c: (0, h)),
    ]
    args = [log_gamma, qkvg, qkvg, qkvg, qkvg, gn_w.reshape(1, H * dv)]
    if s0 is not None:
        in_specs.append(pl.BlockSpec((None, None, dk, dv), lambda b, h, c: (b, h, 0, 0)))
        args.append(s0)
    return pl.pallas_call(
        functools.partial(_retention_body, L=L, zero_init=s0 is None),
        grid=(B, H, nC),
        in_specs=in_specs,
        out_specs=[
            pl.BlockSpec((L, dv), lambda b, h, c: (b * nC + c, h)),
            pl.BlockSpec((None, None, dk, dv), lambda b, h, c: (b, h, 0, 0)),
        ],
        out_shape=[
            jax.ShapeDtypeStruct((B * T, H * dv), BF16),
            jax.ShapeDtypeStruct((B, H, dk, dv), F32),
        ],
        compiler_params=_params("parallel", "parallel", "arbitrary"),
        name="retention",
    )(*args)


def _mlstm_pre_body(x_ref, prev_ref, hist_ref, cw_ref, cb_ref, bdqk_ref, bdv_ref, wg_ref, bg_ref,
                    q_ref, k_ref, v_ref, xc_ref, gate_ref, *, k_scale, n_gate):
    t = pl.program_id(1)
    tm, inner = x_ref.shape
    G = MXU_DIM
    first = t == 0
    row8 = lax.broadcasted_iota(jnp.int32, (8, G), 0)
    gacc = jnp.zeros((tm, GATE_LANES), F32)
    for g in range(inner // G):
        cols = slice(g * G, (g + 1) * G)
        xb = x_ref[:, cols]
        x = xb.astype(F32)
        p = jnp.where(first, hist_ref[:, cols], prev_ref[:, cols]).astype(F32)[BF16_SUBLANES - 8:]
        conv = cb_ref[:, cols] + cw_ref[CONV_WIDTH - 1:CONV_WIDTH, cols] * x
        for s in range(1, CONV_WIDTH):
            xs = pltpu.roll(x, s, 0)
            ps = pltpu.roll(p, s, 0)
            top = jnp.where(row8 < s, ps, xs[:8])
            xs = jnp.concatenate([top, xs[8:]], axis=0) if tm > 8 else top
            conv = conv + cw_ref[CONV_WIDTH - 1 - s:CONV_WIDTH - s, cols] * xs
        xc = _silu(conv).astype(BF16)
        xc_ref[:, cols] = xc
        qk = jnp.dot(xc, bdqk_ref[g], preferred_element_type=F32)
        qg = qk[:, :G].astype(BF16)
        kg = qk[:, G:]
        vg = jnp.dot(xb, bdv_ref[g], preferred_element_type=F32).astype(BF16)
        q_ref[:, cols] = qg
        k_ref[:, cols] = (kg * k_scale).astype(BF16)
        v_ref[:, cols] = vg
        gacc = gacc + jnp.dot(qg, wg_ref[0, cols, :], preferred_element_type=F32)
        gacc = gacc + jnp.dot(kg.astype(BF16), wg_ref[1, cols, :], preferred_element_type=F32)
        gacc = gacc + jnp.dot(vg, wg_ref[2, cols, :], preferred_element_type=F32)
    gates = gacc + bg_ref[...]
    lane = lax.broadcasted_iota(jnp.int32, gates.shape, 1)
    log_sig = jnp.minimum(gates, 0.0) - jnp.log1p(jnp.exp(-jnp.abs(gates)))
    gate_ref[...] = jnp.where(lane >= n_gate, log_sig, gates)


def _mlstm_pre(xz, hist, cw, cb, bdqk, bdv, wg, bg, *, B, T, tm, inner, k_scale):
    nT = T // tm
    pb = BF16_SUBLANES
    row_spec = pl.BlockSpec((tm, inner), lambda b, t: (b * nT + t, 0))
    in_specs = [
        row_spec,
        pl.BlockSpec((pb, inner), lambda b, t: (jnp.maximum((b * nT + t) * (tm // pb) - 1, 0), 0)),
        pl.BlockSpec((None, pb, inner), lambda b, t: (b, 0, 0)),
        pl.BlockSpec((CONV_WIDTH, inner), lambda b, t: (0, 0)),
        pl.BlockSpec((1, inner), lambda b, t: (0, 0)),
        pl.BlockSpec(bdqk.shape, lambda b, t: (0, 0, 0)),
        pl.BlockSpec(bdv.shape, lambda b, t: (0, 0, 0)),
        pl.BlockSpec(wg.shape, lambda b, t: (0, 0, 0)),
        pl.BlockSpec((1, GATE_LANES), lambda b, t: (0, 0)),
    ]
    act = jax.ShapeDtypeStruct((B * T, inner), BF16)
    return pl.pallas_call(
        functools.partial(_mlstm_pre_body, k_scale=k_scale, n_gate=MLSTM_HEADS),
        grid=(B, nT),
        in_specs=in_specs,
        out_specs=[row_spec, row_spec, row_spec, row_spec,
                   pl.BlockSpec((tm, GATE_LANES), lambda b, t: (b * nT + t, 0))],
        out_shape=[act, act, act, act, jax.ShapeDtypeStruct((B * T, GATE_LANES), F32)],
        compiler_params=_params("parallel", "arbitrary"),
        name="mlstm_pre",
    )(xz, xz, hist, cw, cb.reshape(1, inner), bdqk, bdv, wg, bg)


def _mlstm_body(*refs, L, zero_init):
    if zero_init:
        (m0_ref, q_ref, k_ref, v_ref, gate_ref, xc_ref, z_ref, gnw_ref, skip_ref,
         u_ref, c_ref, n_ref, m_ref) = refs
    else:
        (m0_ref, q_ref, k_ref, v_ref, gate_ref, xc_ref, z_ref, gnw_ref, skip_ref, c0_ref, n0_ref,
         u_ref, c_ref, n_ref, m_ref) = refs
    b = pl.program_id(0)
    h = pl.program_id(1)
    c = pl.program_id(2)
    H = MLSTM_HEADS

    @pl.when(c == 0)
    def _():
        if zero_init:
            c_ref[...] = jnp.zeros_like(c_ref)
            n_ref[...] = jnp.zeros_like(n_ref)
        else:
            c_ref[...] = c0_ref[...]
            n_ref[...] = n0_ref[...]
        m_ref[...] = jnp.full(m_ref.shape, m0_ref[b, h], F32)

    q = q_ref[...]
    k = k_ref[...]
    v = v_ref[...]
    gates = gate_ref[...]
    lane = lax.broadcasted_iota(jnp.int32, gates.shape, 1)
    ig_col = jnp.sum(jnp.where(lane == h, gates, 0.0), axis=1, keepdims=True)
    lf_col = jnp.sum(jnp.where(lane == H + h, gates, 0.0), axis=1, keepdims=True)
    ii = lax.broadcasted_iota(jnp.int32, (L, L), 0)
    jj = lax.broadcasted_iota(jnp.int32, (L, L), 1)
    eye = ii == jj
    causal = ii >= jj
    ig_row = jnp.sum(jnp.where(eye, ig_col, 0.0), axis=0, keepdims=True)
    lf_row = jnp.sum(jnp.where(eye, lf_col, 0.0), axis=0, keepdims=True)
    b_col = jnp.sum(jnp.where(causal, lf_row, 0.0), axis=1, keepdims=True)
    b_row = jnp.sum(jnp.where(ii <= jj, lf_col, 0.0), axis=0, keepdims=True)
    m_prev = m_ref[0:1, 0:1]
    log_d = jnp.where(causal, b_col - b_row + ig_row, -jnp.inf)
    log_past = b_col + m_prev
    m_col = jnp.maximum(log_past, jnp.max(log_d, axis=1, keepdims=True))
    d = jnp.exp(log_d - m_col)
    w_past = jnp.exp(log_past - m_col)
    s = lax.dot_general(q, k, (((1,), (1,)), ((), ())), preferred_element_type=F32) * d
    cmat = c_ref[...]
    n_prev = n_ref[...]
    num = (jnp.dot(s.astype(BF16), v, preferred_element_type=F32)
           + jnp.dot(q, cmat.astype(BF16), preferred_element_type=F32) * w_past)
    n_b = jnp.broadcast_to(n_prev, (BF16_SUBLANES, n_prev.shape[1])).astype(BF16)
    qn = lax.dot_general(q, n_b, (((1,), (1,)), ((), ())), preferred_element_type=F32)[:, 0:1]
    den = jnp.sum(s, axis=1, keepdims=True) + qn * w_past
    den = jnp.maximum(jnp.abs(den), jnp.exp(-m_col))
    hh = num * (1.0 / den)

    m_new = m_col[L - 1:L, :]
    b_last = b_col[L - 1:L, :]
    w_tail_row = jnp.exp(b_last - b_row + ig_row - m_new)
    w_tail_col = jnp.sum(jnp.where(eye, w_tail_row, 0.0), axis=1, keepdims=True)
    w_carry = jnp.exp(b_last + m_prev - m_new)
    k_tail = k.astype(F32) * w_tail_col
    c_ref[...] = w_carry * cmat + lax.dot_general(
        k_tail.astype(BF16), v, (((0,), (0,)), ((), ())), preferred_element_type=F32)
    n_ref[...] = w_carry * n_prev + jnp.sum(k_tail, axis=0, keepdims=True)
    m_ref[...] = jnp.broadcast_to(m_new, m_ref.shape)

    hh = hh - jnp.mean(hh, axis=-1, keepdims=True)
    hh = hh * lax.rsqrt(jnp.mean(hh * hh, axis=-1, keepdims=True) + EPS) * gnw_ref[...]
    o = hh + skip_ref[...] * xc_ref[...].astype(F32)
    u_ref[...] = (o * _silu(z_ref[...].astype(F32))).astype(BF16)


def _mlstm(q, k, v, gates, xc, xz, gn_w, skip, m0, c0, n0, *, B, T, L, dh):
    H = MLSTM_HEADS
    nC = T // L
    inner = H * dh
    blk = pl.BlockSpec((L, dh), lambda b, h, c: (b * nC + c, h))
    vec = pl.BlockSpec((1, dh), lambda b, h, c: (0, h))
    c_spec = pl.BlockSpec((None, None, dh, dh), lambda b, h, c: (b, h, 0, 0))
    n_spec = pl.BlockSpec((None, None, 1, dh), lambda b, h, c: (b, h, 0, 0))
    m_spec = pl.BlockSpec((None, None, 8, LANES), lambda b, h, c: (b, h, 0, 0))
    in_specs = [
        pl.BlockSpec(memory_space=pltpu.SMEM),
        blk, blk, blk,
        pl.BlockSpec((L, GATE_LANES), lambda b, h, c: (b * nC + c, 0)),
        blk,
        pl.BlockSpec((L, dh), lambda b, h, c: (b * nC + c, H + h)),
        vec, vec,
    ]
    args = [m0, q, k, v, gates, xc, xz, gn_w.reshape(1, inner), skip.reshape(1, inner)]
    if c0 is not None:
        in_specs += [c_spec, n_spec]
        args += [c0, n0.reshape(B, H, 1, dh)]
    u, c_new, n_new, m_new = pl.pallas_call(
        functools.partial(_mlstm_body, L=L, zero_init=c0 is None),
        grid=(B, H, nC),
        in_specs=in_specs,
        out_specs=[blk, c_spec, n_spec, m_spec],
        out_shape=[
            jax.ShapeDtypeStruct((B * T, inner), BF16),
            jax.ShapeDtypeStruct((B, H, dh, dh), F32),
            jax.ShapeDtypeStruct((B, H, 1, dh), F32),
            jax.ShapeDtypeStruct((B, H, 8, LANES), F32),
        ],
        compiler_params=_params("parallel", "parallel", "arbitrary"),
        name="mlstm",
    )(*args)
    return u, c_new, n_new.reshape(B, H, dh), m_new[:, :, 0, 0]


def _block_diag_groups(w):
    nb, qb, _ = w.shape
    per = MXU_DIM // qb
    w4 = w.reshape(nb // per, per, qb, qb)
    eye = jnp.eye(per, dtype=w.dtype)
    return jnp.einsum("gncd,nm->gncmd", w4, eye).reshape(nb // per, MXU_DIM, MXU_DIM)


def _rope_tables(pos, half):
    inv_freq = ROPE_BASE ** (-jnp.linspace(0.0, 1.0, half, dtype=F32))
    ang = pos.astype(F32)[:, None] * inv_freq[None, :]
    return jnp.cos(ang), jnp.sin(ang)


def _chunk(T):
    return PROMPT_CHUNK if T % PROMPT_CHUNK == 0 else T


def _retention_layer(h2d, B, T, pos, nw, w_in, gn_w, w_out, s0):
    D = h2d.shape[1]
    H = RET_HEADS
    dk = D // H
    dv = 2 * D // H
    M = B * T
    cos, sin = _rope_tables(pos, dk // 2)
    tm = _pick(M, (1024, 512, 256, 128))
    if T % tm:
        cos = jnp.tile(cos, (tm // T, 1))
        sin = jnp.tile(sin, (tm // T, 1))
    tn = _pick(D, (1024, 512, 256))
    qkvg =_norm_proj(h2d, nw, w_in.astype(BF16), tm=tm, tn=tn,
                      rot=(cos, sin, 2 * D, D, dk ** -0.5, dk))
    log_gamma = jnp.log1p(-jnp.exp2(-5.0 - jnp.arange(H, dtype=F32)))
    u, s_new = _retention(qkvg, log_gamma, gn_w, s0, B=B, T=T, L=_chunk(T), dk=dk, dv=dv)
    h_new = _out_proj(u, w_out.astype(BF16), h2d, tm=_pick(M, (256, 128)))
    return h_new, s_new


def _mlstm_layer(h2d, B, T, nw, w_in, conv_w, conv_b, w_q, w_k, w_v, w_ig, b_ig, w_fg, b_fg,
                 skip, gn_w, w_out, norm_f, conv_buf, c0, n0, m0):
    D = h2d.shape[1]
    H = MLSTM_HEADS
    inner = 2 * D
    dh = inner // H
    M = B * T
    xz = _norm_proj(h2d, nw, w_in.astype(BF16), tm=_pick(M, (1024, 512, 256, 128)),
                    tn=_pick(2 * inner, (1024, 512, 256)))
    bdqk = jnp.concatenate([_block_diag_groups(w_q), _block_diag_groups(w_k)], axis=-1).astype(BF16)
    bdv = _block_diag_groups(w_v).astype(BF16)
    wg = jnp.concatenate([w_ig, w_fg], axis=1)
    wg = jnp.pad(wg, ((0, 0), (0, GATE_LANES - 2 * H))).reshape(3, inner, GATE_LANES).astype(BF16)
    bg = jnp.pad(jnp.concatenate([b_ig, b_fg]), (0, GATE_LANES - 2 * H)).reshape(1, GATE_LANES)
    hist = jnp.zeros((B, BF16_SUBLANES, inner), BF16)
    if conv_buf is not None:
        hist = hist.at[:, BF16_SUBLANES - (CONV_WIDTH - 1):].set(conv_buf.astype(BF16))
    q, k, v, xc, gates = _mlstm_pre(xz, hist, conv_w, conv_b, bdqk, bdv, wg, bg, B=B, T=T,
                                    tm=_pick(T, (256, 128)), inner=inner, k_scale=dh ** -0.5)
    u, c_new, n_new, m_new = _mlstm(q, k, v, gates, xc, xz, gn_w, skip, m0, c0, n0,
                                    B=B, T=T, L=_chunk(T), dh=dh)
    y = _out_proj(u, w_out.astype(BF16), h2d, norm_f, tm=_pick(M, (256, 128)))
    conv_new = xz.reshape(B, T, 2 * inner)[:, T - (CONV_WIDTH - 1):, :inner].astype(F32)
    return y, c_new, n_new, m_new, conv_new


def kernel(x_prompt, x_sample, state_ret, state_mlstm_c, state_mlstm_n, state_mlstm_m, state_mlstm_conv, norm_w, ret_w_in, ret_gn_w, ret_w_out, ml_w_in, ml_conv_w, ml_conv_b, ml_w_q, ml_w_k, ml_w_v, ml_w_ig, ml_b_ig, ml_w_fg, ml_b_fg, ml_skip, ml_gn_w, ml_w_out, norm_f):
    Bp, Tp, D = x_prompt.shape
    Bs, Ts, _ = x_sample.shape
    pos_p = jnp.arange(Tp, dtype=jnp.int32)
    pos_s = PAST_LEN + jnp.arange(Ts, dtype=jnp.int32)
    ret_w = (norm_w[0], ret_w_in[0], ret_gn_w[0], ret_w_out[0])
    ml_w = (norm_w[1], ml_w_in[0], ml_conv_w[0], ml_conv_b[0], ml_w_q[0], ml_w_k[0], ml_w_v[0],
            ml_w_ig[0], ml_b_ig[0], ml_w_fg[0], ml_b_fg[0], ml_skip[0], ml_gn_w[0], ml_w_out[0], norm_f)

    hp, ret_p = _retention_layer(x_prompt.reshape(Bp * Tp, D), Bp, Tp, pos_p, *ret_w, None)
    hs, ret_s = _retention_layer(x_sample.reshape(Bs * Ts, D), Bs, Ts, pos_s, *ret_w, state_ret[0])

    m0_p = jnp.full((Bp, MLSTM_HEADS), M_INIT, F32)
    yp, mc_p, mn_p, mm_p, cv_p = _mlstm_layer(hp, Bp, Tp, *ml_w, None, None, None, m0_p)
    ys, mc_s, mn_s, mm_s, cv_s = _mlstm_layer(hs, Bs, Ts, *ml_w, state_mlstm_conv[0],
                                              state_mlstm_c[0], state_mlstm_n[0], state_mlstm_m[0])
    return (yp.reshape(Bp, Tp, D), ys.reshape(Bs, Ts, D),
            ret_p[None], mc_p[None], mn_p[None], mm_p[None], cv_p[None],
            ret_s[None], mc_s[None], mn_s[None], mm_s[None], cv_s[None])
```

```python
import functools

import jax
import jax.numpy as jnp
from jax import lax
from jax.experimental import pallas as pl
from jax.experimental.pallas import tpu as pltpu

F32 = jnp.float32
BF16 = jnp.bfloat16

RET_HEADS = 8
MLSTM_HEADS = 4
CONV_WIDTH = 4
QKV_BLOCK = 4
ROPE_BASE = 10000.0
EPS = 1e-6
M_INIT = -1e30
PAST_LEN = 1024

LANES = 128
MXU_DIM = 256
BF16_SUBLANES = 16
GATE_LANES = 128
VMEM_LIMIT_BYTES = 56 * 1024 * 1024

RET_CHUNK = 256
MLSTM_CHUNK = 512


def _params(*sem):
    return pltpu.CompilerParams(dimension_semantics=sem, vmem_limit_bytes=VMEM_LIMIT_BYTES)


def _pick(n, prefs):
    for p in prefs:
        if n % p == 0:
            return p
    return n


def _silu(x):
    return x * (1.0 / (1.0 + jnp.exp(-x)))


def _norm_proj_body(*refs, rot_tiles, k_tiles, k_scale, head_dim):
    if rot_tiles:
        x_ref, nw_ref, w_ref, cos_ref, sin_ref, o_ref, xn_ref = refs
    else:
        x_ref, nw_ref, w_ref, o_ref, xn_ref = refs
    j = pl.program_id(1)

    @pl.when(j == 0)
    def _():
        x = x_ref[...]
        ms = jnp.mean(x * x, axis=-1, keepdims=True)
        xn_ref[...] = (x * lax.rsqrt(ms + EPS) * nw_ref[...]).astype(BF16)

    acc = jnp.dot(xn_ref[...], w_ref[...], preferred_element_type=F32)
    if not rot_tiles:
        o_ref[...] = acc.astype(o_ref.dtype)
        return

    @pl.when(j < rot_tiles)
    def _():
        cos = cos_ref[...]
        sin = sin_ref[...]
        scale = jnp.where(j >= rot_tiles - k_tiles, k_scale, 1.0).astype(F32)
        half = head_dim // 2
        for hh in range(acc.shape[1] // head_dim):
            lo = hh * head_dim
            x1 = acc[:, lo:lo + half]
            x2 = acc[:, lo + half:lo + head_dim]
            o_ref[:, lo:lo + half] = ((x1 * cos - x2 * sin) * scale).astype(o_ref.dtype)
            o_ref[:, lo + half:lo + head_dim] = ((x1 * sin + x2 * cos) * scale).astype(o_ref.dtype)

    @pl.when(j >= rot_tiles)
    def _():
        o_ref[...] = acc.astype(o_ref.dtype)


def _norm_proj(x, nw, w, *, tm, tn, rot=None):
    M, D = x.shape
    N = w.shape[1]
    grid = (M // tm, N // tn)
    in_specs = [
        pl.BlockSpec((tm, D), lambda i, j: (i, 0)),
        pl.BlockSpec((1, D), lambda i, j: (0, 0)),
        pl.BlockSpec((D, tn), lambda i, j: (0, j)),
    ]
    args = [x, nw.reshape(1, D), w]
    kw = dict(rot_tiles=0, k_tiles=0, k_scale=1.0, head_dim=0)
    if rot is not None:
        cos, sin, qk_cols, k_cols, k_scale, head_dim = rot
        period = cos.shape[0] // tm
        half = head_dim // 2
        in_specs += [pl.BlockSpec((tm, half), lambda i, j: (i % period, 0))] * 2
        args += [cos, sin]
        kw = dict(rot_tiles=qk_cols // tn, k_tiles=k_cols // tn, k_scale=k_scale, head_dim=head_dim)
    return pl.pallas_call(
        functools.partial(_norm_proj_body, **kw),
        grid=grid,
        in_specs=in_specs,
        out_specs=pl.BlockSpec((tm, tn), lambda i, j: (i, j)),
        out_shape=jax.ShapeDtypeStruct((M, N), BF16),
        scratch_shapes=[pltpu.VMEM((tm, D), BF16)],
        compiler_params=_params("parallel", "arbitrary"),
        name="norm_proj_rot" if rot is not None else "norm_proj",
    )(*args)


def _out_proj_body(*refs, final_norm):
    if final_norm:
        u_ref, w_ref, h_ref, nf_ref, o_ref = refs
    else:
        u_ref, w_ref, h_ref, o_ref = refs
    y = h_ref[...] + jnp.dot(u_ref[...], w_ref[...], preferred_element_type=F32)
    if final_norm:
        ms = jnp.mean(y * y, axis=-1, keepdims=True)
        y = y * lax.rsqrt(ms + EPS) * nf_ref[...]
    o_ref[...] = y


def _out_proj(u, w, h, nf=None, *, tm):
    M, K = u.shape
    N = w.shape[1]
    in_specs = [
        pl.BlockSpec((tm, K), lambda i: (i, 0)),
        pl.BlockSpec((K, N), lambda i: (0, 0)),
        pl.BlockSpec((tm, N), lambda i: (i, 0)),
    ]
    args = [u, w, h]
    if nf is not None:
        in_specs.append(pl.BlockSpec((1, N), lambda i: (0, 0)))
        args.append(nf.reshape(1, N))
    return pl.pallas_call(
        functools.partial(_out_proj_body, final_norm=nf is not None),
        grid=(M // tm,),
        in_specs=in_specs,
        out_specs=pl.BlockSpec((tm, N), lambda i: (i, 0)),
        out_shape=jax.ShapeDtypeStruct((M, N), F32),
        compiler_params=_params("parallel"),
        name="out_proj_norm" if nf is not None else "out_proj",
    )(*args)


def _retention_body(*refs, L, H, dk, dv, zero_init):
    if zero_init:
        lg_ref, q_ref, k_ref, v_ref, g_ref, gnw_ref, u_ref, s_ref, decay_ref = refs
    else:
        lg_ref, q_ref, k_ref, v_ref, g_ref, gnw_ref, s0_ref, u_ref, s_ref, decay_ref = refs
    c = pl.program_id(1)
    idx = lax.broadcasted_iota(jnp.int32, (L, 1), 0).astype(F32)

    @pl.when(c == 0)
    def _():
        if zero_init:
            s_ref[...] = jnp.zeros_like(s_ref)
        else:
            s_ref[...] = s0_ref[...]
        ii = lax.broadcasted_iota(jnp.int32, (L, L), 0)
        jj = lax.broadcasted_iota(jnp.int32, (L, L), 1)
        rel = jnp.maximum(ii - jj, 0).astype(F32)
        for h in range(H):
            decay_ref[h] = jnp.where(ii >= jj, jnp.exp(lg_ref[h] * rel), 0.0)

    for h in range(H):
        lg = jnp.full((1, 1), lg_ref[h], F32)
        q = q_ref[:, h * dk:(h + 1) * dk]
        k = k_ref[:, h * dk:(h + 1) * dk]
        v = v_ref[:, h * dv:(h + 1) * dv]
        scores = lax.dot_general(q, k, (((1,), (1,)), ((), ())), preferred_element_type=F32) * decay_ref[h]
        intra = jnp.dot(scores.astype(BF16), v, preferred_element_type=F32)
        s = s_ref[h]
        cross = jnp.dot(q, s.astype(BF16), preferred_element_type=F32) * jnp.exp(lg * (idx + 1.0))
        o = intra + cross
        k_tail = (k.astype(F32) * jnp.exp(lg * (L - 1.0 - idx))).astype(BF16)
        s_ref[h] = jnp.exp(lg * float(L)) * s + lax.dot_general(
            k_tail, v, (((0,), (0,)), ((), ())), preferred_element_type=F32)
        o = o * lax.rsqrt(jnp.mean(o * o, axis=-1, keepdims=True) + EPS) * gnw_ref[:, h * dv:(h + 1) * dv]
        g = g_ref[:, h * dv:(h + 1) * dv].astype(F32)
        u_ref[:, h * dv:(h + 1) * dv] = (_silu(g) * o).astype(BF16)


def _retention(qkvg, log_gamma, gn_w, s0, *, B, T, L, dk, dv):
    H = RET_HEADS
    nC = T // L
    qk_w, v_w = H * dk, H * dv
    in_specs = [
        pl.BlockSpec(memory_space=pltpu.SMEM),
        pl.BlockSpec((L, qk_w), lambda b, c: (b * nC + c, 0)),
        pl.BlockSpec((L, qk_w), lambda b, c: (b * nC + c, 1)),
        pl.BlockSpec((L, v_w), lambda b, c: (b * nC + c, 2 * qk_w // v_w)),
        pl.BlockSpec((L, v_w), lambda b, c: (b * nC + c, 2 * qk_w // v_w + 1)),
        pl.BlockSpec((1, v_w), lambda b, c: (0, 0)),
    ]
    args = [log_gamma, qkvg, qkvg, qkvg, qkvg, gn_w.reshape(1, v_w)]
    s_spec = pl.BlockSpec((None, H, dk, dv), lambda b, c: (b, 0, 0, 0))
    if s0 is not None:
        in_specs.append(s_spec)
        args.append(s0)
    return pl.pallas_call(
        functools.partial(_retention_body, L=L, H=H, dk=dk, dv=dv, zero_init=s0 is None),
        grid=(B, nC),
        in_specs=in_specs,
        out_specs=[pl.BlockSpec((L, v_w), lambda b, c: (b * nC + c, 0)), s_spec],
        out_shape=[
            jax.ShapeDtypeStruct((B * T, v_w), BF16),
            jax.ShapeDtypeStruct((B, H, dk, dv), F32),
        ],
        scratch_shapes=[pltpu.VMEM((H, L, L), F32)],
        compiler_params=_params("parallel", "arbitrary"),
        name="retention",
    )(*args)


def _mlstm_pre_body(x_ref, prev_ref, hist_ref, cw_ref, cb_ref, bdqk_ref, bdv_ref, wg_ref, bg_ref,
                    q_ref, k_ref, v_ref, xc_ref, gate_ref, *, k_scale, n_gate):
    t = pl.program_id(1)
    tm, inner = x_ref.shape
    G = MXU_DIM
    first = t == 0
    row8 = lax.broadcasted_iota(jnp.int32, (8, G), 0)
    gacc = jnp.zeros((tm, GATE_LANES), F32)
    for g in range(inner // G):
        cols = slice(g * G, (g + 1) * G)
        xb = x_ref[:, cols]
        x = xb.astype(F32)
        p = jnp.where(first, hist_ref[:, cols], prev_ref[:, cols]).astype(F32)[BF16_SUBLANES - 8:]
        conv = cb_ref[:, cols] + cw_ref[CONV_WIDTH - 1:CONV_WIDTH, cols] * x
        for s in range(1, CONV_WIDTH):
            xs = pltpu.roll(x, s, 0)
            ps = pltpu.roll(p, s, 0)
            top = jnp.where(row8 < s, ps, xs[:8])
            xs = jnp.concatenate([top, xs[8:]], axis=0) if tm > 8 else top
            conv = conv + cw_ref[CONV_WIDTH - 1 - s:CONV_WIDTH - s, cols] * xs
        xc = _silu(conv).astype(BF16)
        xc_ref[:, cols] = xc
        qk = jnp.dot(xc, bdqk_ref[g], preferred_element_type=F32)
        q_ref[:, cols] = qk[:, :G].astype(BF16)
        k_ref[:, cols] = (qk[:, G:] * k_scale).astype(BF16)
        v_ref[:, cols] = jnp.dot(xb, bdv_ref[g], preferred_element_type=F32).astype(BF16)
        gacc = gacc + jnp.dot(xc, wg_ref[0, cols, :], preferred_element_type=F32)
        gacc = gacc + jnp.dot(xb, wg_ref[1, cols, :], preferred_element_type=F32)
    gates = gacc + bg_ref[...]
    lane = lax.broadcasted_iota(jnp.int32, gates.shape, 1)
    log_sig = jnp.minimum(gates, 0.0) - jnp.log1p(jnp.exp(-jnp.abs(gates)))
    gate_ref[...] = jnp.where(lane >= n_gate, log_sig, gates)


def _mlstm_pre(xz, hist, cw, cb, bdqk, bdv, wg, bg, *, B, T, tm, inner, k_scale):
    nT = T // tm
    pb = BF16_SUBLANES
    row_spec = pl.BlockSpec((tm, inner), lambda b, t: (b * nT + t, 0))
    in_specs = [
        row_spec,
        pl.BlockSpec((pb, inner), lambda b, t: (jnp.maximum((b * nT + t) * (tm // pb) - 1, 0), 0)),
        pl.BlockSpec((None, pb, inner), lambda b, t: (b, 0, 0)),
        pl.BlockSpec((CONV_WIDTH, inner), lambda b, t: (0, 0)),
        pl.BlockSpec((1, inner), lambda b, t: (0, 0)),
        pl.BlockSpec(bdqk.shape, lambda b, t: (0, 0, 0)),
        pl.BlockSpec(bdv.shape, lambda b, t: (0, 0, 0)),
        pl.BlockSpec(wg.shape, lambda b, t: (0, 0, 0)),
        pl.BlockSpec((1, GATE_LANES), lambda b, t: (0, 0)),
    ]
    act = jax.ShapeDtypeStruct((B * T, inner), BF16)
    return pl.pallas_call(
        functools.partial(_mlstm_pre_body, k_scale=k_scale, n_gate=MLSTM_HEADS),
        grid=(B, nT),
        in_specs=in_specs,
        out_specs=[row_spec, row_spec, row_spec, row_spec,
                   pl.BlockSpec((tm, GATE_LANES), lambda b, t: (b * nT + t, 0))],
        out_shape=[act, act, act, act, jax.ShapeDtypeStruct((B * T, GATE_LANES), F32)],
        compiler_params=_params("parallel", "arbitrary"),
        name="mlstm_pre",
    )(xz, xz, hist, cw, cb.reshape(1, inner), bdqk, bdv, wg, bg)


def _mlstm_body(*refs, L, zero_init):
    if zero_init:
        (m0_ref, q_ref, k_ref, v_ref, gate_ref, xc_ref, z_ref, gnw_ref, skip_ref,
         u_ref, c_ref, n_ref, m_ref) = refs
    else:
        (m0_ref, q_ref, k_ref, v_ref, gate_ref, xc_ref, z_ref, gnw_ref, skip_ref, c0_ref, n0_ref,
         u_ref, c_ref, n_ref, m_ref) = refs
    b = pl.program_id(0)
    h = pl.program_id(1)
    c = pl.program_id(2)
    H = MLSTM_HEADS

    @pl.when(c == 0)
    def _():
        if zero_init:
            c_ref[...] = jnp.zeros_like(c_ref)
            n_ref[...] = jnp.zeros_like(n_ref)
        else:
            c_ref[...] = c0_ref[...]
            n_ref[...] = n0_ref[...]
        m_ref[...] = jnp.full(m_ref.shape, m0_ref[b, h], F32)

    q = q_ref[...]
    k = k_ref[...]
    v = v_ref[...]
    gates = gate_ref[...]
    lane = lax.broadcasted_iota(jnp.int32, gates.shape, 1)
    ig_col = jnp.sum(jnp.where(lane == h, gates, 0.0), axis=1, keepdims=True)
    lf_col = jnp.sum(jnp.where(lane == H + h, gates, 0.0), axis=1, keepdims=True)
    ii = lax.broadcasted_iota(jnp.int32, (L, L), 0)
    jj = lax.broadcasted_iota(jnp.int32, (L, L), 1)
    eye = ii == jj
    causal = ii >= jj
    ig_row = jnp.sum(jnp.where(eye, ig_col, 0.0), axis=0, keepdims=True)
    lf_row = jnp.sum(jnp.where(eye, lf_col, 0.0), axis=0, keepdims=True)
    b_col = jnp.sum(jnp.where(causal, lf_row, 0.0), axis=1, keepdims=True)
    b_row = jnp.sum(jnp.where(ii <= jj, lf_col, 0.0), axis=0, keepdims=True)
    m_prev = m_ref[0:1, 0:1]
    log_d = jnp.where(causal, b_col - b_row + ig_row, -jnp.inf)
    log_past = b_col + m_prev
    m_col = jnp.maximum(log_past, jnp.max(log_d, axis=1, keepdims=True))
    d = jnp.exp(log_d - m_col)
    w_past = jnp.exp(log_past - m_col)
    s = lax.dot_general(q, k, (((1,), (1,)), ((), ())), preferred_element_type=F32) * d
    cmat = c_ref[...]
    n_prev = n_ref[...]
    num = (jnp.dot(s.astype(BF16), v, preferred_element_type=F32)
           + jnp.dot(q, cmat.astype(BF16), preferred_element_type=F32) * w_past)
    n_b = jnp.broadcast_to(n_prev, (BF16_SUBLANES, n_prev.shape[1])).astype(BF16)
    qn = lax.dot_general(q, n_b, (((1,), (1,)), ((), ())), preferred_element_type=F32)[:, 0:1]
    den = jnp.sum(s, axis=1, keepdims=True) + qn * w_past
    den = jnp.maximum(jnp.abs(den), jnp.exp(-m_col))
    hh = num * (1.0 / den)

    m_new = m_col[L - 1:L, :]
    b_last = b_col[L - 1:L, :]
    w_tail_row = jnp.exp(b_last - b_row + ig_row - m_new)
    w_tail_col = jnp.sum(jnp.where(eye, w_tail_row, 0.0), axis=1, keepdims=True)
    w_carry = jnp.exp(b_last + m_prev - m_new)
    k_tail = k.astype(F32) * w_tail_col
    c_ref[...] = w_carry * cmat + lax.dot_general(
        k_tail.astype(BF16), v, (((0,), (0,)), ((), ())), preferred_element_type=F32)
    n_ref[...] = w_carry * n_prev + jnp.sum(k_tail, axis=0, keepdims=True)
    m_ref[...] = jnp.broadcast_to(m_new, m_ref.shape)

    hh = hh - jnp.mean(hh, axis=-1, keepdims=True)
    hh = hh * lax.rsqrt(jnp.mean(hh * hh, axis=-1, keepdims=True) + EPS) * gnw_ref[...]
    o = hh + skip_ref[...] * xc_ref[...].astype(F32)
    u_ref[...] = (o * _silu(z_ref[...].astype(F32))).astype(BF16)


def _mlstm(q, k, v, gates, xc, xz, gn_w, skip, m0, c0, n0, *, B, T, L, dh):
    H = MLSTM_HEADS
    nC = T // L
    inner = H * dh
    blk = pl.BlockSpec((L, dh), lambda b, h, c: (b * nC + c, h))
    vec = pl.BlockSpec((1, dh), lambda b, h, c: (0, h))
    c_spec = pl.BlockSpec((None, None, dh, dh), lambda b, h, c: (b, h, 0, 0))
    n_spec = pl.BlockSpec((None, None, 1, dh), lambda b, h, c: (b, h, 0, 0))
    m_spec = pl.BlockSpec((None, None, 8, LANES), lambda b, h, c: (b, h, 0, 0))
    in_specs = [
        pl.BlockSpec(memory_space=pltpu.SMEM),
        blk, blk, blk,
        pl.BlockSpec((L, GATE_LANES), lambda b, h, c: (b * nC + c, 0)),
        blk,
        pl.BlockSpec((L, dh), lambda b, h, c: (b * nC + c, H + h)),
        vec, vec,
    ]
    args = [m0, q, k, v, gates, xc, xz, gn_w.reshape(1, inner), skip.reshape(1, inner)]
    if c0 is not None:
        in_specs += [c_spec, n_spec]
        args += [c0, n0.reshape(B, H, 1, dh)]
    u, c_new, n_new, m_new = pl.pallas_call(
        functools.partial(_mlstm_body, L=L, zero_init=c0 is None),
        grid=(B, H, nC),
        in_specs=in_specs,
        out_specs=[blk, c_spec, n_spec, m_spec],
        out_shape=[
            jax.ShapeDtypeStruct((B * T, inner), BF16),
            jax.ShapeDtypeStruct((B, H, dh, dh), F32),
            jax.ShapeDtypeStruct((B, H, 1, dh), F32),
            jax.ShapeDtypeStruct((B, H, 8, LANES), F32),
        ],
        compiler_params=_params("parallel", "parallel", "arbitrary"),
        name="mlstm",
    )(*args)
    return u, c_new, n_new.reshape(B, H, dh), m_new[:, :, 0, 0]


def _block_diag_groups(w):
    nb, qb, _ = w.shape
    per = MXU_DIM // qb
    w4 = w.reshape(nb // per, per, qb, qb)
    eye = jnp.eye(per, dtype=w.dtype)
    return jnp.einsum("gncd,nm->gncmd", w4, eye).reshape(nb // per, MXU_DIM, MXU_DIM)


def _fold_block_diag(w, dense):
    nb, qb, _ = w.shape
    return jnp.einsum("ncd,ndk->nck", w, dense.reshape(nb, qb, -1),
                      precision=lax.Precision.HIGHEST).reshape(nb * qb, -1)


def _rope_tables(pos, half):
    inv_freq = ROPE_BASE ** (-jnp.linspace(0.0, 1.0, half, dtype=F32))
    ang = pos.astype(F32)[:, None] * inv_freq[None, :]
    return jnp.cos(ang), jnp.sin(ang)


def _chunk(T, pref):
    return pref if T % pref == 0 else T


def _retention_layer(h2d, B, T, pos, nw, w_in, gn_w, w_out, s0):
    D = h2d.shape[1]
    H = RET_HEADS
    dk = D // H
    dv = 2 * D // H
    M = B * T
    cos, sin = _rope_tables(pos, dk // 2)
    tm = _pick(M, (1024, 512, 256, 128))
    if T % tm:
        cos = jnp.tile(cos, (tm // T, 1))
        sin = jnp.tile(sin, (tm // T, 1))
    tn = _pick(D, (1024, 512, 256))
    qkvg =_norm_proj(h2d, nw, w_in.astype(BF16), tm=tm, tn=tn,
                      rot=(cos, sin, 2 * D, D, dk ** -0.5, dk))
    log_gamma = jnp.log1p(-jnp.exp2(-5.0 - jnp.arange(H, dtype=F32)))
    u, s_new = _retention(qkvg, log_gamma, gn_w, s0, B=B, T=T, L=_chunk(T, RET_CHUNK), dk=dk, dv=dv)
    h_new = _out_proj(u, w_out.astype(BF16), h2d, tm=_pick(M, (256, 128)))
    return h_new, s_new


def _mlstm_layer(h2d, B, T, nw, w_in, conv_w, conv_b, w_q, w_k, w_v, w_ig, b_ig, w_fg, b_fg,
                 skip, gn_w, w_out, norm_f, conv_buf, c0, n0, m0):
    D = h2d.shape[1]
    H = MLSTM_HEADS
    inner = 2 * D
    dh = inner // H
    M = B * T
    xz = _norm_proj(h2d, nw, w_in.astype(BF16), tm=_pick(M, (1024, 512, 256, 128)),
                    tn=_pick(2 * inner, (1024, 512, 256)))
    bdqk = jnp.concatenate([_block_diag_groups(w_q), _block_diag_groups(w_k)], axis=-1).astype(BF16)
    bdv = _block_diag_groups(w_v).astype(BF16)
    wg = jnp.concatenate([w_ig, w_fg], axis=1).reshape(3, inner, 2 * H)
    wg = jnp.stack([_fold_block_diag(w_q, wg[0]) + _fold_block_diag(w_k, wg[1]), _fold_block_diag(w_v, wg[2])])
    wg = jnp.pad(wg, ((0, 0), (0, 0), (0, GATE_LANES - 2 * H))).astype(BF16)
    bg = jnp.pad(jnp.concatenate([b_ig, b_fg]), (0, GATE_LANES - 2 * H)).reshape(1, GATE_LANES)
    hist = jnp.zeros((B, BF16_SUBLANES, inner), BF16)
    if conv_buf is not None:
        hist = hist.at[:, BF16_SUBLANES - (CONV_WIDTH - 1):].set(conv_buf.astype(BF16))
    q, k, v, xc, gates = _mlstm_pre(xz, hist, conv_w, conv_b, bdqk, bdv, wg, bg, B=B, T=T,
                                    tm=_pick(T, (256, 128)), inner=inner, k_scale=dh ** -0.5)
    u, c_new, n_new, m_new = _mlstm(q, k, v, gates, xc, xz, gn_w, skip, m0, c0, n0,
                                    B=B, T=T, L=_chunk(T, MLSTM_CHUNK), dh=dh)
    y = _out_proj(u, w_out.astype(BF16), h2d, norm_f, tm=_pick(M, (256, 128)))
    conv_new = xz.reshape(B, T, 2 * inner)[:, T - (CONV_WIDTH - 1):, :inner].astype(F32)
    return y, c_new, n_new, m_new, conv_new


def kernel(x_prompt, x_sample, state_ret, state_mlstm_c, state_mlstm_n, state_mlstm_m, state_mlstm_conv, norm_w, ret_w_in, ret_gn_w, ret_w_out, ml_w_in, ml_conv_w, ml_conv_b, ml_w_q, ml_w_k, ml_w_v, ml_w_ig, ml_b_ig, ml_w_fg, ml_b_fg, ml_skip, ml_gn_w, ml_w_out, norm_f):
    Bp, Tp, D = x_prompt.shape
    Bs, Ts, _ = x_sample.shape
    pos_p = jnp.arange(Tp, dtype=jnp.int32)
    pos_s = PAST_LEN + jnp.arange(Ts, dtype=jnp.int32)
    ret_w = (norm_w[0], ret_w_in[0], ret_gn_w[0], ret_w_out[0])
    ml_w = (norm_w[1], ml_w_in[0], ml_conv_w[0], ml_conv_b[0], ml_w_q[0], ml_w_k[0], ml_w_v[0],
            ml_w_ig[0], ml_b_ig[0], ml_w_fg[0], ml_b_fg[0], ml_skip[0], ml_gn_w[0], ml_w_out[0], norm_f)

    hp, ret_p = _retention_layer(x_prompt.reshape(Bp * Tp, D), Bp, Tp, pos_p, *ret_w, None)
    hs, ret_s = _retention_layer(x_sample.reshape(Bs * Ts, D), Bs, Ts, pos_s, *ret_w, state_ret[0])

    m0_p = jnp.full((Bp, MLSTM_HEADS), M_INIT, F32)
    yp, mc_p, mn_p, mm_p, cv_p = _mlstm_layer(hp, Bp, Tp, *ml_w, None, None, None, m0_p)
    ys, mc_s, mn_s, mm_s, cv_s = _mlstm_layer(hs, Bs, Ts, *ml_w, state_mlstm_conv[0],
                                              state_mlstm_c[0], state_mlstm_n[0], state_mlstm_m[0])
    return (yp.reshape(Bp, Tp, D), ys.reshape(Bs, Ts, D),
            ret_p[None], mc_p[None], mn_p[None], mm_p[None], cv_p[None],
            ret_s[None], mc_s[None], mn_s[None], mm_s[None], cv_s[None])
```

```python
import functools

import jax
import jax.numpy as jnp
from jax import lax
from jax.experimental import pallas as pl
from jax.experimental.pallas import tpu as pltpu

F32 = jnp.float32
BF16 = jnp.bfloat16

RET_HEADS = 8
MLSTM_HEADS = 4
CONV_WIDTH = 4
QKV_BLOCK = 4
ROPE_BASE = 10000.0
EPS = 1e-6
M_INIT = -1e30
PAST_LEN = 1024

LANES = 128
MXU_DIM = 256
BF16_SUBLANES = 16
GATE_LANES = 128
VMEM_LIMIT_BYTES = 56 * 1024 * 1024

RET_CHUNK = 256
MLSTM_CHUNK = 512


def _params(*sem):
    return pltpu.CompilerParams(dimension_semantics=sem, vmem_limit_bytes=VMEM_LIMIT_BYTES)


def _pick(n, prefs):
    for p in prefs:
        if n % p == 0:
            return p
    return n


def _silu(x):
    return x * (1.0 / (1.0 + jnp.exp(-x)))


def _norm_proj_body(x_ref, nw_ref, w_ref, cos_ref, sin_ref, o_ref, xn_ref, *,
                    rot_tiles, k_tiles, k_scale, head_dim):
    j = pl.program_id(1)

    @pl.when(j == 0)
    def _():
        x = x_ref[...]
        ms = jnp.mean(x * x, axis=-1, keepdims=True)
        xn_ref[...] = (x * lax.rsqrt(ms + EPS) * nw_ref[...]).astype(BF16)

    acc = jnp.dot(xn_ref[...], w_ref[...], preferred_element_type=F32)

    @pl.when(j < rot_tiles)
    def _():
        cos = cos_ref[...]
        sin = sin_ref[...]
        scale = jnp.where(j >= rot_tiles - k_tiles, k_scale, 1.0).astype(F32)
        half = head_dim // 2
        for hh in range(acc.shape[1] // head_dim):
            lo = hh * head_dim
            x1 = acc[:, lo:lo + half]
            x2 = acc[:, lo + half:lo + head_dim]
            o_ref[:, lo:lo + half] = ((x1 * cos - x2 * sin) * scale).astype(o_ref.dtype)
            o_ref[:, lo + half:lo + head_dim] = ((x1 * sin + x2 * cos) * scale).astype(o_ref.dtype)

    @pl.when(j >= rot_tiles)
    def _():
        o_ref[...] = acc.astype(o_ref.dtype)


def _norm_proj(x, nw, w, cos, sin, *, tm, tn, qk_cols, k_cols, k_scale, head_dim):
    M, D = x.shape
    N = w.shape[1]
    period = cos.shape[0] // tm
    half = head_dim // 2
    tab = pl.BlockSpec((tm, half), lambda i, j: (i % period, 0))
    return pl.pallas_call(
        functools.partial(_norm_proj_body, rot_tiles=qk_cols // tn, k_tiles=k_cols // tn,
                          k_scale=k_scale, head_dim=head_dim),
        grid=(M // tm, N // tn),
        in_specs=[
            pl.BlockSpec((tm, D), lambda i, j: (i, 0)),
            pl.BlockSpec((1, D), lambda i, j: (0, 0)),
            pl.BlockSpec((D, tn), lambda i, j: (0, j)),
            tab, tab,
        ],
        out_specs=pl.BlockSpec((tm, tn), lambda i, j: (i, j)),
        out_shape=jax.ShapeDtypeStruct((M, N), BF16),
        scratch_shapes=[pltpu.VMEM((tm, D), BF16)],
        compiler_params=_params("parallel", "arbitrary"),
        name="norm_proj_rot",
    )(x, nw.reshape(1, D), w, cos, sin)


def _proj_body(x_ref, w_ref, o_ref):
    o_ref[...] = jnp.dot(x_ref[...], w_ref[...], preferred_element_type=F32).astype(o_ref.dtype)


def _proj(x, w, *, tm, tn):
    M, D = x.shape
    N = w.shape[1]
    return pl.pallas_call(
        _proj_body,
        grid=(M // tm, N // tn),
        in_specs=[pl.BlockSpec((tm, D), lambda i, j: (i, 0)), pl.BlockSpec((D, tn), lambda i, j: (0, j))],
        out_specs=pl.BlockSpec((tm, tn), lambda i, j: (i, j)),
        out_shape=jax.ShapeDtypeStruct((M, N), BF16),
        compiler_params=_params("parallel", "arbitrary"),
        name="proj",
    )(x, w)


def _out_proj_body(u_ref, w_ref, h_ref, nw_ref, *o_refs, final):
    y = h_ref[...] + jnp.dot(u_ref[...], w_ref[...], preferred_element_type=F32)
    ms = jnp.mean(y * y, axis=-1, keepdims=True)
    yn = y * lax.rsqrt(ms + EPS) * nw_ref[...]
    if final:
        o_refs[0][...] = yn
    else:
        o_refs[0][...] = y
        o_refs[1][...] = yn.astype(BF16)


def _out_proj(u, w, h, nw, *, tm, final):
    M, K = u.shape
    N = w.shape[1]
    row = pl.BlockSpec((tm, N), lambda i: (i, 0))
    in_specs = [
        pl.BlockSpec((tm, K), lambda i: (i, 0)),
        pl.BlockSpec((K, N), lambda i: (0, 0)),
        row,
        pl.BlockSpec((1, N), lambda i: (0, 0)),
    ]
    y_shape = jax.ShapeDtypeStruct((M, N), F32)
    return pl.pallas_call(
        functools.partial(_out_proj_body, final=final),
        grid=(M // tm,),
        in_specs=in_specs,
        out_specs=row if final else [row, row],
        out_shape=y_shape if final else [y_shape, jax.ShapeDtypeStruct((M, N), BF16)],
        compiler_params=_params("parallel"),
        name="out_proj_final" if final else "out_proj",
    )(u, w, h, nw.reshape(1, N))


def _retention_body(*refs, L, H, dk, dv, zero_init):
    if zero_init:
        lg_ref, q_ref, k_ref, v_ref, g_ref, u_ref, s_ref, decay_ref = refs
    else:
        lg_ref, q_ref, k_ref, v_ref, g_ref, s0_ref, u_ref, s_ref, decay_ref = refs
    c = pl.program_id(1)
    idx = lax.broadcasted_iota(jnp.int32, (L, 1), 0).astype(F32)

    @pl.when(c == 0)
    def _():
        if zero_init:
            s_ref[...] = jnp.zeros_like(s_ref)
        else:
            s_ref[...] = s0_ref[...]
        ii = lax.broadcasted_iota(jnp.int32, (L, L), 0)
        jj = lax.broadcasted_iota(jnp.int32, (L, L), 1)
        rel = jnp.maximum(ii - jj, 0).astype(F32)
        for h in range(H):
            decay_ref[h] = jnp.where(ii >= jj, jnp.exp(lg_ref[h] * rel), 0.0)

    for h in range(H):
        lg = jnp.full((1, 1), lg_ref[h], F32)
        q = q_ref[:, h * dk:(h + 1) * dk]
        k = k_ref[:, h * dk:(h + 1) * dk]
        v = v_ref[:, h * dv:(h + 1) * dv]
        scores = lax.dot_general(q, k, (((1,), (1,)), ((), ())), preferred_element_type=F32) * decay_ref[h]
        intra = jnp.dot(scores.astype(BF16), v, preferred_element_type=F32)
        s = s_ref[h]
        cross = jnp.dot(q, s.astype(BF16), preferred_element_type=F32) * jnp.exp(lg * (idx + 1.0))
        o = intra + cross
        k_tail = (k.astype(F32) * jnp.exp(lg * (L - 1.0 - idx))).astype(BF16)
        s_ref[h] = jnp.exp(lg * float(L)) * s + lax.dot_general(
            k_tail, v, (((0,), (0,)), ((), ())), preferred_element_type=F32)
        o = o * lax.rsqrt(jnp.mean(o * o, axis=-1, keepdims=True) + EPS)
        g = g_ref[:, h * dv:(h + 1) * dv].astype(F32)
        u_ref[:, h * dv:(h + 1) * dv] = (_silu(g) * o).astype(BF16)


def _retention(qkvg, log_gamma, s0, *, B, T, L, dk, dv):
    H = RET_HEADS
    nC = T // L
    qk_w, v_w = H * dk, H * dv
    in_specs = [
        pl.BlockSpec(memory_space=pltpu.SMEM),
        pl.BlockSpec((L, qk_w), lambda b, c: (b * nC + c, 0)),
        pl.BlockSpec((L, qk_w), lambda b, c: (b * nC + c, 1)),
        pl.BlockSpec((L, v_w), lambda b, c: (b * nC + c, 2 * qk_w // v_w)),
        pl.BlockSpec((L, v_w), lambda b, c: (b * nC + c, 2 * qk_w // v_w + 1)),
    ]
    args = [log_gamma, qkvg, qkvg, qkvg, qkvg]
    s_spec = pl.BlockSpec((None, H, dk, dv), lambda b, c: (b, 0, 0, 0))
    if s0 is not None:
        in_specs.append(s_spec)
        args.append(s0)
    return pl.pallas_call(
        functools.partial(_retention_body, L=L, H=H, dk=dk, dv=dv, zero_init=s0 is None),
        grid=(B, nC),
        in_specs=in_specs,
        out_specs=[pl.BlockSpec((L, v_w), lambda b, c: (b * nC + c, 0)), s_spec],
        out_shape=[
            jax.ShapeDtypeStruct((B * T, v_w), BF16),
            jax.ShapeDtypeStruct((B, H, dk, dv), F32),
        ],
        scratch_shapes=[pltpu.VMEM((H, L, L), F32)],
        compiler_params=_params("parallel", "arbitrary"),
        name="retention",
    )(*args)


def _mlstm_pre_body(x_ref, prev_ref, hist_ref, cw_ref, cb_ref, bdqk_ref, bdv_ref, wg_ref, bg_ref,
                    q_ref, k_ref, v_ref, xc_ref, gate_ref, *, n_gate):
    t = pl.program_id(1)
    tm, inner = x_ref.shape
    G = MXU_DIM
    first = t == 0
    row8 = lax.broadcasted_iota(jnp.int32, (8, G), 0)
    gates = bg_ref[...]
    for g in range(inner // G):
        cols = slice(g * G, (g + 1) * G)
        xb = x_ref[:, cols]
        x = xb.astype(F32)
        p = jnp.where(first, hist_ref[:, cols], prev_ref[:, cols]).astype(F32)[BF16_SUBLANES - 8:]
        conv = cb_ref[:, cols] + cw_ref[CONV_WIDTH - 1:CONV_WIDTH, cols] * x
        for s in range(1, CONV_WIDTH):
            xs = pltpu.roll(x, s, 0)
            ps = pltpu.roll(p, s, 0)
            top = jnp.where(row8 < s, ps, xs[:8])
            xs = jnp.concatenate([top, xs[8:]], axis=0) if tm > 8 else top
            conv = conv + cw_ref[CONV_WIDTH - 1 - s:CONV_WIDTH - s, cols] * xs
        xc = _silu(conv).astype(BF16)
        xc_ref[:, cols] = xc
        qk = jnp.dot(xc, bdqk_ref[g], preferred_element_type=F32)
        q_ref[:, cols] = qk[:, :G].astype(BF16)
        k_ref[:, cols] = qk[:, G:].astype(BF16)
        v_ref[:, cols] = jnp.dot(xb, bdv_ref[g], preferred_element_type=F32).astype(BF16)
        gates = gates + jnp.dot(jnp.concatenate([xc, xb], axis=1), wg_ref[g], preferred_element_type=F32)
    lane = lax.broadcasted_iota(jnp.int32, gates.shape, 1)
    log_sig = jnp.minimum(gates, 0.0) - jnp.log1p(jnp.exp(-jnp.abs(gates)))
    gate_ref[...] = jnp.where(lane >= n_gate, log_sig, gates)


def _mlstm_pre(xz, hist, cw, cb, bdqk, bdv, wg, bg, *, B, T, tm, inner):
    nT = T // tm
    pb = BF16_SUBLANES
    row_spec = pl.BlockSpec((tm, inner), lambda b, t: (b * nT + t, 0))
    in_specs = [
        row_spec,
        pl.BlockSpec((pb, inner), lambda b, t: (jnp.maximum((b * nT + t) * (tm // pb) - 1, 0), 0)),
        pl.BlockSpec((None, pb, inner), lambda b, t: (b, 0, 0)),
        pl.BlockSpec((CONV_WIDTH, inner), lambda b, t: (0, 0)),
        pl.BlockSpec((1, inner), lambda b, t: (0, 0)),
        pl.BlockSpec(bdqk.shape, lambda b, t: (0, 0, 0)),
        pl.BlockSpec(bdv.shape, lambda b, t: (0, 0, 0)),
        pl.BlockSpec(wg.shape, lambda b, t: (0, 0, 0)),
        pl.BlockSpec((1, GATE_LANES), lambda b, t: (0, 0)),
    ]
    act = jax.ShapeDtypeStruct((B * T, inner), BF16)
    return pl.pallas_call(
        functools.partial(_mlstm_pre_body, n_gate=MLSTM_HEADS),
        grid=(B, nT),
        in_specs=in_specs,
        out_specs=[row_spec, row_spec, row_spec, row_spec,
                   pl.BlockSpec((tm, GATE_LANES), lambda b, t: (b * nT + t, 0))],
        out_shape=[act, act, act, act, jax.ShapeDtypeStruct((B * T, GATE_LANES), F32)],
        compiler_params=_params("parallel", "arbitrary"),
        name="mlstm_pre",
    )(xz, xz, hist, cw, cb.reshape(1, inner), bdqk, bdv, wg, bg)


def _mlstm_body(*refs, L, zero_init):
    if zero_init:
        (m0_ref, q_ref, k_ref, v_ref, gate_ref, xc_ref, z_ref, gnw_ref, skip_ref,
         u_ref, c_ref, n_ref, m_ref) = refs
    else:
        (m0_ref, q_ref, k_ref, v_ref, gate_ref, xc_ref, z_ref, gnw_ref, skip_ref, c0_ref, n0_ref,
         u_ref, c_ref, n_ref, m_ref) = refs
    b = pl.program_id(0)
    h = pl.program_id(1)
    c = pl.program_id(2)
    H = MLSTM_HEADS

    @pl.when(c == 0)
    def _():
        if zero_init:
            c_ref[...] = jnp.zeros_like(c_ref)
            n_ref[...] = jnp.zeros_like(n_ref)
        else:
            c_ref[...] = c0_ref[...]
            n_ref[...] = n0_ref[...]
        m_ref[...] = jnp.full(m_ref.shape, m0_ref[b, h], F32)

    q = q_ref[...]
    k = k_ref[...]
    v = v_ref[...]
    gates = gate_ref[...]
    lane = lax.broadcasted_iota(jnp.int32, gates.shape, 1)
    ig_col = jnp.sum(jnp.where(lane == h, gates, 0.0), axis=1, keepdims=True)
    lf_col = jnp.sum(jnp.where(lane == H + h, gates, 0.0), axis=1, keepdims=True)
    ii = lax.broadcasted_iota(jnp.int32, (L, L), 0)
    jj = lax.broadcasted_iota(jnp.int32, (L, L), 1)
    eye = ii == jj
    causal = ii >= jj
    ig_row = jnp.sum(jnp.where(eye, ig_col, 0.0), axis=0, keepdims=True)
    lf_row = jnp.sum(jnp.where(eye, lf_col, 0.0), axis=0, keepdims=True)
    b_col = jnp.sum(jnp.where(causal, lf_row, 0.0), axis=1, keepdims=True)
    b_row = jnp.sum(jnp.where(ii <= jj, lf_col, 0.0), axis=0, keepdims=True)
    m_prev = m_ref[0:1, 0:1]
    log_d = jnp.where(causal, b_col - b_row + ig_row, -jnp.inf)
    log_past = b_col + m_prev
    m_col = jnp.maximum(log_past, jnp.max(log_d, axis=1, keepdims=True))
    d = jnp.exp(log_d - m_col)
    w_past = jnp.exp(log_past - m_col)
    s = lax.dot_general(q, k, (((1,), (1,)), ((), ())), preferred_element_type=F32) * d
    cmat = c_ref[...]
    n_prev = n_ref[...]
    num = (jnp.dot(s.astype(BF16), v, preferred_element_type=F32)
           + jnp.dot(q, cmat.astype(BF16), preferred_element_type=F32) * w_past)
    n_b = jnp.broadcast_to(n_prev, (BF16_SUBLANES, n_prev.shape[1])).astype(BF16)
    qn = lax.dot_general(q, n_b, (((1,), (1,)), ((), ())), preferred_element_type=F32)[:, 0:1]
    den = jnp.sum(s, axis=1, keepdims=True) + qn * w_past
    den = jnp.maximum(jnp.abs(den), jnp.exp(-m_col))
    hh = num * (1.0 / den)

    m_new = m_col[L - 1:L, :]
    b_last = b_col[L - 1:L, :]
    w_tail_row = jnp.exp(b_last - b_row + ig_row - m_new)
    w_tail_col = jnp.sum(jnp.where(eye, w_tail_row, 0.0), axis=1, keepdims=True)
    w_carry = jnp.exp(b_last + m_prev - m_new)
    k_tail = k.astype(F32) * w_tail_col
    c_ref[...] = w_carry * cmat + lax.dot_general(
        k_tail.astype(BF16), v, (((0,), (0,)), ((), ())), preferred_element_type=F32)
    n_ref[...] = w_carry * n_prev + jnp.sum(k_tail, axis=0, keepdims=True)
    m_ref[...] = jnp.broadcast_to(m_new, m_ref.shape)

    hh = hh - jnp.mean(hh, axis=-1, keepdims=True)
    hh = hh * lax.rsqrt(jnp.mean(hh * hh, axis=-1, keepdims=True) + EPS) * gnw_ref[...]
    o = hh + skip_ref[...] * xc_ref[...].astype(F32)
    u_ref[...] = (o * _silu(z_ref[...].astype(F32))).astype(BF16)


def _mlstm(q, k, v, gates, xc, xz, gn_w, skip, m0, c0, n0, *, B, T, L, dh):
    H = MLSTM_HEADS
    nC = T // L
    inner = H * dh
    blk = pl.BlockSpec((L, dh), lambda b, h, c: (b * nC + c, h))
    vec = pl.BlockSpec((1, dh), lambda b, h, c: (0, h))
    c_spec = pl.BlockSpec((None, None, dh, dh), lambda b, h, c: (b, h, 0, 0))
    n_spec = pl.BlockSpec((None, None, 1, dh), lambda b, h, c: (b, h, 0, 0))
    m_spec = pl.BlockSpec((None, None, 8, LANES), lambda b, h, c: (b, h, 0, 0))
    in_specs = [
        pl.BlockSpec(memory_space=pltpu.SMEM),
        blk, blk, blk,
        pl.BlockSpec((L, GATE_LANES), lambda b, h, c: (b * nC + c, 0)),
        blk,
        pl.BlockSpec((L, dh), lambda b, h, c: (b * nC + c, H + h)),
        vec, vec,
    ]
    args = [m0, q, k, v, gates, xc, xz, gn_w.reshape(1, inner), skip.reshape(1, inner)]
    if c0 is not None:
        in_specs += [c_spec, n_spec]
        args += [c0, n0.reshape(B, H, 1, dh)]
    u, c_new, n_new, m_new = pl.pallas_call(
        functools.partial(_mlstm_body, L=L, zero_init=c0 is None),
        grid=(B, H, nC),
        in_specs=in_specs,
        out_specs=[blk, c_spec, n_spec, m_spec],
        out_shape=[
            jax.ShapeDtypeStruct((B * T, inner), BF16),
            jax.ShapeDtypeStruct((B, H, dh, dh), F32),
            jax.ShapeDtypeStruct((B, H, 1, dh), F32),
            jax.ShapeDtypeStruct((B, H, 8, LANES), F32),
        ],
        compiler_params=_params("parallel", "parallel", "arbitrary"),
        name="mlstm",
    )(*args)
    return u, c_new, n_new.reshape(B, H, dh), m_new[:, :, 0, 0]


def _block_diag_groups(w):
    nb, qb, _ = w.shape
    per = MXU_DIM // qb
    w4 = w.reshape(nb // per, per, qb, qb)
    eye = jnp.eye(per, dtype=w.dtype)
    return jnp.einsum("gncd,nm->gncmd", w4, eye).reshape(nb // per, MXU_DIM, MXU_DIM)


def _fold_block_diag(w, dense):
    nb, qb, _ = w.shape
    return jnp.einsum("ncd,ndk->nck", w, dense.reshape(nb, qb, -1),
                      precision=lax.Precision.HIGHEST).reshape(nb * qb, -1)


def _rope_tables(pos, half):
    inv_freq = ROPE_BASE ** (-jnp.linspace(0.0, 1.0, half, dtype=F32))
    ang = pos.astype(F32)[:, None] * inv_freq[None, :]
    return jnp.cos(ang), jnp.sin(ang)


def _chunk(T, pref):
    return pref if T % pref == 0 else T


def _retention_layer(h2d, B, T, pos, nw, w_in, gn_w, w_out, nw_next, s0):
    D = h2d.shape[1]
    H = RET_HEADS
    dk = D // H
    dv = 2 * D // H
    M = B * T
    cos, sin = _rope_tables(pos, dk // 2)
    tm = _pick(M, (1024, 512, 256, 128))
    if T % tm:
        cos = jnp.tile(cos, (tm // T, 1))
        sin = jnp.tile(sin, (tm // T, 1))
    tn = _pick(D, (1024, 512, 256))
    qkvg = _norm_proj(h2d, nw, w_in.astype(BF16), cos, sin, tm=tm, tn=tn,
                      qk_cols=2 * D, k_cols=D, k_scale=dk ** -0.5, head_dim=dk)
    log_gamma = jnp.log1p(-jnp.exp2(-5.0 - jnp.arange(H, dtype=F32)))
    u, s_new = _retention(qkvg, log_gamma, s0, B=B, T=T, L=_chunk(T, RET_CHUNK), dk=dk, dv=dv)
    h_new, xn_next = _out_proj(u, (gn_w[:, None] * w_out).astype(BF16), h2d, nw_next,
                               tm=_pick(M, (256, 128)), final=False)
    return h_new, xn_next, s_new


def _mlstm_layer(h2d, xn, B, T, w_in, conv_w, conv_b, w_q, w_k, w_v, w_ig, b_ig, w_fg, b_fg,
                 skip, gn_w, w_out, norm_f, conv_buf, c0, n0, m0):
    D = h2d.shape[1]
    H = MLSTM_HEADS
    inner = 2 * D
    dh = inner // H
    M = B * T
    xz = _proj(xn, w_in.astype(BF16), tm=_pick(M, (1024, 512, 256, 128)),
               tn=_pick(2 * inner, (2048, 1024, 512, 256)))
    bdqk = jnp.concatenate([_block_diag_groups(w_q), _block_diag_groups(w_k) * dh ** -0.5], axis=-1).astype(BF16)
    bdv = _block_diag_groups(w_v).astype(BF16)
    wg = jnp.concatenate([w_ig, w_fg], axis=1).reshape(3, inner, 2 * H)
    wg = jnp.stack([_fold_block_diag(w_q, wg[0]) + _fold_block_diag(w_k, wg[1]), _fold_block_diag(w_v, wg[2])])
    wg = jnp.pad(wg, ((0, 0), (0, 0), (0, GATE_LANES - 2 * H))).astype(BF16)
    wg = wg.reshape(2, inner // MXU_DIM, MXU_DIM, GATE_LANES).transpose(1, 0, 2, 3).reshape(
        inner // MXU_DIM, 2 * MXU_DIM, GATE_LANES)
    bg = jnp.pad(jnp.concatenate([b_ig, b_fg]), (0, GATE_LANES - 2 * H)).reshape(1, GATE_LANES)
    hist = jnp.zeros((B, BF16_SUBLANES, inner), BF16)
    if conv_buf is not None:
        hist = hist.at[:, BF16_SUBLANES - (CONV_WIDTH - 1):].set(conv_buf.astype(BF16))
    q, k, v, xc, gates = _mlstm_pre(xz, hist, conv_w, conv_b, bdqk, bdv, wg, bg, B=B, T=T,
                                    tm=_pick(T, (256, 128)), inner=inner)
    u, c_new, n_new, m_new = _mlstm(q, k, v, gates, xc, xz, gn_w, skip, m0, c0, n0,
                                    B=B, T=T, L=_chunk(T, MLSTM_CHUNK), dh=dh)
    y = _out_proj(u, w_out.astype(BF16), h2d, norm_f, tm=_pick(M, (256, 128)), final=True)
    conv_new = xz.reshape(B, T, 2 * inner)[:, T - (CONV_WIDTH - 1):, :inner].astype(F32)
    return y, c_new, n_new, m_new, conv_new


def kernel(x_prompt, x_sample, state_ret, state_mlstm_c, state_mlstm_n, state_mlstm_m, state_mlstm_conv, norm_w, ret_w_in, ret_gn_w, ret_w_out, ml_w_in, ml_conv_w, ml_conv_b, ml_w_q, ml_w_k, ml_w_v, ml_w_ig, ml_b_ig, ml_w_fg, ml_b_fg, ml_skip, ml_gn_w, ml_w_out, norm_f):
    Bp, Tp, D = x_prompt.shape
    Bs, Ts, _ = x_sample.shape
    pos_p = jnp.arange(Tp, dtype=jnp.int32)
    pos_s = PAST_LEN + jnp.arange(Ts, dtype=jnp.int32)
    ret_w = (norm_w[0], ret_w_in[0], ret_gn_w[0], ret_w_out[0], norm_w[1])
    ml_w = (ml_w_in[0], ml_conv_w[0], ml_conv_b[0], ml_w_q[0], ml_w_k[0], ml_w_v[0],
            ml_w_ig[0], ml_b_ig[0], ml_w_fg[0], ml_b_fg[0], ml_skip[0], ml_gn_w[0], ml_w_out[0], norm_f)

    hp, xnp, ret_p = _retention_layer(x_prompt.reshape(Bp * Tp, D), Bp, Tp, pos_p, *ret_w, None)
    hs, xns, ret_s = _retention_layer(x_sample.reshape(Bs * Ts, D), Bs, Ts, pos_s, *ret_w, state_ret[0])

    m0_p = jnp.full((Bp, MLSTM_HEADS), M_INIT, F32)
    yp, mc_p, mn_p, mm_p, cv_p = _mlstm_layer(hp, xnp, Bp, Tp, *ml_w, None, None, None, m0_p)
    ys, mc_s, mn_s, mm_s, cv_s = _mlstm_layer(hs, xns, Bs, Ts, *ml_w, state_mlstm_conv[0],
                                              state_mlstm_c[0], state_mlstm_n[0], state_mlstm_m[0])
    return (yp.reshape(Bp, Tp, D), ys.reshape(Bs, Ts, D),
            ret_p[None], mc_p[None], mn_p[None], mm_p[None], cv_p[None],
            ret_s[None], mc_s[None], mn_s[None], mm_s[None], cv_s[None])
```

```python
import functools

import jax
import jax.numpy as jnp
from jax import lax
from jax.experimental import pallas as pl
from jax.experimental.pallas import tpu as pltpu

F32 = jnp.float32
BF16 = jnp.bfloat16

RET_HEADS = 8
MLSTM_HEADS = 4
CONV_WIDTH = 4
QKV_BLOCK = 4
ROPE_BASE = 10000.0
EPS = 1e-6
M_INIT = -1e30
PAST_LEN = 1024

LANES = 128
MXU_DIM = 256
BF16_SUBLANES = 16
GATE_LANES = 128
VMEM_LIMIT_BYTES = 56 * 1024 * 1024

RET_CHUNK = 256
MLSTM_CHUNK = 512


def _params(*sem):
    return pltpu.CompilerParams(dimension_semantics=sem, vmem_limit_bytes=VMEM_LIMIT_BYTES)


def _pick(n, prefs):
    for p in prefs:
        if n % p == 0:
            return p
    return n


def _silu(x):
    return x * (1.0 / (1.0 + jnp.exp(-x)))


def _norm_proj_body(x_ref, nw_ref, w_ref, cos_ref, sin_ref, o_ref, xn_ref, *,
                    rot_tiles, k_tiles, k_scale, head_dim):
    j = pl.program_id(1)

    @pl.when(j == 0)
    def _():
        x = x_ref[...]
        ms = jnp.mean(x * x, axis=-1, keepdims=True)
        xn_ref[...] = (x * lax.rsqrt(ms + EPS) * nw_ref[...]).astype(BF16)

    half = head_dim // 2

    def head_acc(hh):
        return jnp.dot(xn_ref[...], w_ref[:, hh * head_dim:(hh + 1) * head_dim], preferred_element_type=F32)

    @pl.when(j < rot_tiles)
    def _():
        scale = jnp.where(j >= rot_tiles - k_tiles, k_scale, 1.0).astype(F32)
        cos = cos_ref[...] * scale
        sin = sin_ref[...] * scale
        for hh in range(o_ref.shape[1] // head_dim):
            lo = hh * head_dim
            acc = head_acc(hh)
            x1 = acc[:, :half]
            x2 = acc[:, half:]
            o_ref[:, lo:lo + half] = (x1 * cos - x2 * sin).astype(o_ref.dtype)
            o_ref[:, lo + half:lo + head_dim] = (x1 * sin + x2 * cos).astype(o_ref.dtype)

    @pl.when(j >= rot_tiles)
    def _():
        for hh in range(o_ref.shape[1] // head_dim):
            o_ref[:, hh * head_dim:(hh + 1) * head_dim] = head_acc(hh).astype(o_ref.dtype)


def _norm_proj(x, nw, w, cos, sin, *, tm, tn, qk_cols, k_cols, k_scale, head_dim):
    M, D = x.shape
    N = w.shape[1]
    period = cos.shape[0] // tm
    half = head_dim // 2
    tab = pl.BlockSpec((tm, half), lambda i, j: (i % period, 0))
    return pl.pallas_call(
        functools.partial(_norm_proj_body, rot_tiles=qk_cols // tn, k_tiles=k_cols // tn,
                          k_scale=k_scale, head_dim=head_dim),
        grid=(M // tm, N // tn),
        in_specs=[
            pl.BlockSpec((tm, D), lambda i, j: (i, 0)),
            pl.BlockSpec((1, D), lambda i, j: (0, 0)),
            pl.BlockSpec((D, tn), lambda i, j: (0, j)),
            tab, tab,
        ],
        out_specs=pl.BlockSpec((tm, tn), lambda i, j: (i, j)),
        out_shape=jax.ShapeDtypeStruct((M, N), BF16),
        scratch_shapes=[pltpu.VMEM((tm, D), BF16)],
        compiler_params=_params("parallel", "arbitrary"),
        name="norm_proj_rot",
    )(x, nw.reshape(1, D), w, cos, sin)


def _proj_body(x_ref, w_ref, o_ref):
    o_ref[...] = jnp.dot(x_ref[...], w_ref[...], preferred_element_type=F32).astype(o_ref.dtype)


def _proj(x, w, *, tm, tn):
    M, D = x.shape
    N = w.shape[1]
    return pl.pallas_call(
        _proj_body,
        grid=(M // tm, N // tn),
        in_specs=[pl.BlockSpec((tm, D), lambda i, j: (i, 0)), pl.BlockSpec((D, tn), lambda i, j: (0, j))],
        out_specs=pl.BlockSpec((tm, tn), lambda i, j: (i, j)),
        out_shape=jax.ShapeDtypeStruct((M, N), BF16),
        compiler_params=_params("parallel", "arbitrary"),
        name="proj",
    )(x, w)


def _out_proj_body(u_ref, w_ref, h_ref, nw_ref, *o_refs, final):
    y = h_ref[...] + jnp.dot(u_ref[...], w_ref[...], preferred_element_type=F32)
    ms = jnp.mean(y * y, axis=-1, keepdims=True)
    yn = y * lax.rsqrt(ms + EPS) * nw_ref[...]
    if final:
        o_refs[0][...] = yn
    else:
        o_refs[0][...] = y
        o_refs[1][...] = yn.astype(BF16)


def _out_proj(u, w, h, nw, *, tm, final):
    M, K = u.shape
    N = w.shape[1]
    row = pl.BlockSpec((tm, N), lambda i: (i, 0))
    in_specs = [
        pl.BlockSpec((tm, K), lambda i: (i, 0)),
        pl.BlockSpec((K, N), lambda i: (0, 0)),
        row,
        pl.BlockSpec((1, N), lambda i: (0, 0)),
    ]
    y_shape = jax.ShapeDtypeStruct((M, N), F32)
    return pl.pallas_call(
        functools.partial(_out_proj_body, final=final),
        grid=(M // tm,),
        in_specs=in_specs,
        out_specs=row if final else [row, row],
        out_shape=y_shape if final else [y_shape, jax.ShapeDtypeStruct((M, N), BF16)],
        compiler_params=_params("parallel"),
        name="out_proj_final" if final else "out_proj",
    )(u, w, h, nw.reshape(1, N))


def _retention_body(*refs, L, H, dk, dv, zero_init):
    if zero_init:
        lg_ref, q_ref, k_ref, v_ref, g_ref, u_ref, s_ref, decay_ref = refs
    else:
        lg_ref, q_ref, k_ref, v_ref, g_ref, s0_ref, u_ref, s_ref, decay_ref = refs
    c = pl.program_id(1)
    idx = lax.broadcasted_iota(jnp.int32, (L, 1), 0).astype(F32)

    @pl.when(c == 0)
    def _():
        if zero_init:
            s_ref[...] = jnp.zeros_like(s_ref)
        else:
            s_ref[...] = s0_ref[...]
        ii = lax.broadcasted_iota(jnp.int32, (L, L), 0)
        jj = lax.broadcasted_iota(jnp.int32, (L, L), 1)
        rel = jnp.maximum(ii - jj, 0).astype(F32)
        for h in range(H):
            decay_ref[h] = jnp.where(ii >= jj, jnp.exp(lg_ref[h] * rel), 0.0)

    for h in range(H):
        lg = jnp.full((1, 1), lg_ref[h], F32)
        q = q_ref[:, h * dk:(h + 1) * dk]
        k = k_ref[:, h * dk:(h + 1) * dk]
        v = v_ref[:, h * dv:(h + 1) * dv]
        scores = lax.dot_general(q, k, (((1,), (1,)), ((), ())), preferred_element_type=F32) * decay_ref[h]
        intra = jnp.dot(scores.astype(BF16), v, preferred_element_type=F32)
        s = s_ref[h]
        cross = jnp.dot(q, s.astype(BF16), preferred_element_type=F32) * jnp.exp(lg * (idx + 1.0))
        o = intra + cross
        k_tail = (k.astype(F32) * jnp.exp(lg * (L - 1.0 - idx))).astype(BF16)
        s_ref[h] = jnp.exp(lg * float(L)) * s + lax.dot_general(
            k_tail, v, (((0,), (0,)), ((), ())), preferred_element_type=F32)
        o = o * lax.rsqrt(jnp.mean(o * o, axis=-1, keepdims=True) + EPS)
        g = g_ref[:, h * dv:(h + 1) * dv].astype(F32)
        u_ref[:, h * dv:(h + 1) * dv] = (_silu(g) * o).astype(BF16)


def _retention(qkvg, log_gamma, s0, *, B, T, L, dk, dv):
    H = RET_HEADS
    nC = T // L
    qk_w, v_w = H * dk, H * dv
    in_specs = [
        pl.BlockSpec(memory_space=pltpu.SMEM),
        pl.BlockSpec((L, qk_w), lambda b, c: (b * nC + c, 0)),
        pl.BlockSpec((L, qk_w), lambda b, c: (b * nC + c, 1)),
        pl.BlockSpec((L, v_w), lambda b, c: (b * nC + c, 2 * qk_w // v_w)),
        pl.BlockSpec((L, v_w), lambda b, c: (b * nC + c, 2 * qk_w // v_w + 1)),
    ]
    args = [log_gamma, qkvg, qkvg, qkvg, qkvg]
    s_spec = pl.BlockSpec((None, H, dk, dv), lambda b, c: (b, 0, 0, 0))
    if s0 is not None:
        in_specs.append(s_spec)
        args.append(s0)
    return pl.pallas_call(
        functools.partial(_retention_body, L=L, H=H, dk=dk, dv=dv, zero_init=s0 is None),
        grid=(B, nC),
        in_specs=in_specs,
        out_specs=[pl.BlockSpec((L, v_w), lambda b, c: (b * nC + c, 0)), s_spec],
        out_shape=[
            jax.ShapeDtypeStruct((B * T, v_w), BF16),
            jax.ShapeDtypeStruct((B, H, dk, dv), F32),
        ],
        scratch_shapes=[pltpu.VMEM((H, L, L), F32)],
        compiler_params=_params("parallel", "arbitrary"),
        name="retention",
    )(*args)


def _mlstm_pre_body(x_ref, prev_ref, hist_ref, cw_ref, cb_ref, bdqk_ref, bdv_ref, wg_ref, bg_ref,
                    q_ref, k_ref, v_ref, xc_ref, gate_ref, *, n_gate):
    t = pl.program_id(1)
    tm, inner = x_ref.shape
    G = MXU_DIM
    first = t == 0
    row8 = lax.broadcasted_iota(jnp.int32, (8, G), 0)
    gates = bg_ref[...]
    for g in range(inner // G):
        cols = slice(g * G, (g + 1) * G)
        xb = x_ref[:, cols]
        x = xb.astype(F32)
        p = jnp.where(first, hist_ref[:, cols], prev_ref[:, cols]).astype(F32)[BF16_SUBLANES - 8:]
        conv = cb_ref[:, cols] + cw_ref[CONV_WIDTH - 1:CONV_WIDTH, cols] * x
        for s in range(1, CONV_WIDTH):
            xs = pltpu.roll(x, s, 0)
            ps = pltpu.roll(p, s, 0)
            top = jnp.where(row8 < s, ps, xs[:8])
            xs = jnp.concatenate([top, xs[8:]], axis=0) if tm > 8 else top
            conv = conv + cw_ref[CONV_WIDTH - 1 - s:CONV_WIDTH - s, cols] * xs
        xc = _silu(conv).astype(BF16)
        xc_ref[:, cols] = xc
        qk = jnp.dot(xc, bdqk_ref[g], preferred_element_type=F32)
        q_ref[:, cols] = qk[:, :G].astype(BF16)
        k_ref[:, cols] = qk[:, G:].astype(BF16)
        v_ref[:, cols] = jnp.dot(xb, bdv_ref[g], preferred_element_type=F32).astype(BF16)
        gates = gates + jnp.dot(jnp.concatenate([xc, xb], axis=1), wg_ref[g], preferred_element_type=F32)
    lane = lax.broadcasted_iota(jnp.int32, gates.shape, 1)
    log_sig = jnp.minimum(gates, 0.0) - jnp.log1p(jnp.exp(-jnp.abs(gates)))
    gate_ref[...] = jnp.where(lane >= n_gate, log_sig, gates)


def _mlstm_pre(xz, hist, cw, cb, bdqk, bdv, wg, bg, *, B, T, tm, inner):
    nT = T // tm
    pb = BF16_SUBLANES
    row_spec = pl.BlockSpec((tm, inner), lambda b, t: (b * nT + t, 0))
    in_specs = [
        row_spec,
        pl.BlockSpec((pb, inner), lambda b, t: (jnp.maximum((b * nT + t) * (tm // pb) - 1, 0), 0)),
        pl.BlockSpec((None, pb, inner), lambda b, t: (b, 0, 0)),
        pl.BlockSpec((CONV_WIDTH, inner), lambda b, t: (0, 0)),
        pl.BlockSpec((1, inner), lambda b, t: (0, 0)),
        pl.BlockSpec(bdqk.shape, lambda b, t: (0, 0, 0)),
        pl.BlockSpec(bdv.shape, lambda b, t: (0, 0, 0)),
        pl.BlockSpec(wg.shape, lambda b, t: (0, 0, 0)),
        pl.BlockSpec((1, GATE_LANES), lambda b, t: (0, 0)),
    ]
    act = jax.ShapeDtypeStruct((B * T, inner), BF16)
    return pl.pallas_call(
        functools.partial(_mlstm_pre_body, n_gate=MLSTM_HEADS),
        grid=(B, nT),
        in_specs=in_specs,
        out_specs=[row_spec, row_spec, row_spec, row_spec,
                   pl.BlockSpec((tm, GATE_LANES), lambda b, t: (b * nT + t, 0))],
        out_shape=[act, act, act, act, jax.ShapeDtypeStruct((B * T, GATE_LANES), F32)],
        compiler_params=_params("parallel", "arbitrary"),
        name="mlstm_pre",
    )(xz, xz, hist, cw, cb.reshape(1, inner), bdqk, bdv, wg, bg)


def _mlstm_body(*refs, L, zero_init):
    if zero_init:
        (m0_ref, q_ref, k_ref, v_ref, gate_ref, xc_ref, z_ref, gnw_ref, skip_ref,
         u_ref, c_ref, n_ref, m_ref) = refs
    else:
        (m0_ref, q_ref, k_ref, v_ref, gate_ref, xc_ref, z_ref, gnw_ref, skip_ref, c0_ref, n0_ref,
         u_ref, c_ref, n_ref, m_ref) = refs
    b = pl.program_id(0)
    h = pl.program_id(1)
    c = pl.program_id(2)
    H = MLSTM_HEADS

    @pl.when(c == 0)
    def _():
        if zero_init:
            c_ref[...] = jnp.zeros_like(c_ref)
            n_ref[...] = jnp.zeros_like(n_ref)
        else:
            c_ref[...] = c0_ref[...]
            n_ref[...] = n0_ref[...]
        m_ref[...] = jnp.full(m_ref.shape, m0_ref[b, h], F32)

    q = q_ref[...]
    k = k_ref[...]
    v = v_ref[...]
    gates = gate_ref[...]
    lane = lax.broadcasted_iota(jnp.int32, gates.shape, 1)
    ig_col = jnp.sum(jnp.where(lane == h, gates, 0.0), axis=1, keepdims=True)
    lf_col = jnp.sum(jnp.where(lane == H + h, gates, 0.0), axis=1, keepdims=True)
    ii = lax.broadcasted_iota(jnp.int32, (L, L), 0)
    jj = lax.broadcasted_iota(jnp.int32, (L, L), 1)
    eye = ii == jj
    causal = ii >= jj
    ig_row = jnp.sum(jnp.where(eye, ig_col, 0.0), axis=0, keepdims=True)
    lf_row = jnp.sum(jnp.where(eye, lf_col, 0.0), axis=0, keepdims=True)
    b_col = jnp.sum(jnp.where(causal, lf_row, 0.0), axis=1, keepdims=True)
    b_row = jnp.sum(jnp.where(ii <= jj, lf_col, 0.0), axis=0, keepdims=True)
    m_prev = m_ref[0:1, 0:1]
    log_d = jnp.where(causal, b_col - b_row + ig_row, -jnp.inf)
    log_past = b_col + m_prev
    m_col = jnp.maximum(log_past, jnp.max(log_d, axis=1, keepdims=True))
    d = jnp.exp(log_d - m_col)
    w_past = jnp.exp(log_past - m_col)
    s = lax.dot_general(q, k, (((1,), (1,)), ((), ())), preferred_element_type=F32) * d
    cmat = c_ref[...]
    n_prev = n_ref[...]
    num = (jnp.dot(s.astype(BF16), v, preferred_element_type=F32)
           + jnp.dot(q, cmat.astype(BF16), preferred_element_type=F32) * w_past)
    n_b = jnp.broadcast_to(n_prev, (BF16_SUBLANES, n_prev.shape[1])).astype(BF16)
    qn = lax.dot_general(q, n_b, (((1,), (1,)), ((), ())), preferred_element_type=F32)[:, 0:1]
    den = jnp.sum(s, axis=1, keepdims=True) + qn * w_past
    den = jnp.maximum(jnp.abs(den), jnp.exp(-m_col))
    hh = num * (1.0 / den)

    m_new = m_col[L - 1:L, :]
    b_last = b_col[L - 1:L, :]
    w_tail_row = jnp.exp(b_last - b_row + ig_row - m_new)
    w_tail_col = jnp.sum(jnp.where(eye, w_tail_row, 0.0), axis=1, keepdims=True)
    w_carry = jnp.exp(b_last + m_prev - m_new)
    k_tail = k.astype(F32) * w_tail_col
    c_ref[...] = w_carry * cmat + lax.dot_general(
        k_tail.astype(BF16), v, (((0,), (0,)), ((), ())), preferred_element_type=F32)
    n_ref[...] = w_carry * n_prev + jnp.sum(k_tail, axis=0, keepdims=True)
    m_ref[...] = jnp.broadcast_to(m_new, m_ref.shape)

    hh = hh - jnp.mean(hh, axis=-1, keepdims=True)
    hh = hh * lax.rsqrt(jnp.mean(hh * hh, axis=-1, keepdims=True) + EPS) * gnw_ref[...]
    o = hh + skip_ref[...] * xc_ref[...].astype(F32)
    u_ref[...] = (o * _silu(z_ref[...].astype(F32))).astype(BF16)


def _mlstm(q, k, v, gates, xc, xz, gn_w, skip, m0, c0, n0, *, B, T, L, dh):
    H = MLSTM_HEADS
    nC = T // L
    inner = H * dh
    blk = pl.BlockSpec((L, dh), lambda b, h, c: (b * nC + c, h))
    vec = pl.BlockSpec((1, dh), lambda b, h, c: (0, h))
    c_spec = pl.BlockSpec((None, None, dh, dh), lambda b, h, c: (b, h, 0, 0))
    n_spec = pl.BlockSpec((None, None, 1, dh), lambda b, h, c: (b, h, 0, 0))
    m_spec = pl.BlockSpec((None, None, 8, LANES), lambda b, h, c: (b, h, 0, 0))
    in_specs = [
        pl.BlockSpec(memory_space=pltpu.SMEM),
        blk, blk, blk,
        pl.BlockSpec((L, GATE_LANES), lambda b, h, c: (b * nC + c, 0)),
        blk,
        pl.BlockSpec((L, dh), lambda b, h, c: (b * nC + c, H + h)),
        vec, vec,
    ]
    args = [m0, q, k, v, gates, xc, xz, gn_w.reshape(1, inner), skip.reshape(1, inner)]
    if c0 is not None:
        in_specs += [c_spec, n_spec]
        args += [c0, n0.reshape(B, H, 1, dh)]
    u, c_new, n_new, m_new = pl.pallas_call(
        functools.partial(_mlstm_body, L=L, zero_init=c0 is None),
        grid=(B, H, nC),
        in_specs=in_specs,
        out_specs=[blk, c_spec, n_spec, m_spec],
        out_shape=[
            jax.ShapeDtypeStruct((B * T, inner), BF16),
            jax.ShapeDtypeStruct((B, H, dh, dh), F32),
            jax.ShapeDtypeStruct((B, H, 1, dh), F32),
            jax.ShapeDtypeStruct((B, H, 8, LANES), F32),
        ],
        compiler_params=_params("parallel", "parallel", "arbitrary"),
        name="mlstm",
    )(*args)
    return u, c_new, n_new.reshape(B, H, dh), m_new[:, :, 0, 0]


def _block_diag_groups(w):
    nb, qb, _ = w.shape
    per = MXU_DIM // qb
    w4 = w.reshape(nb // per, per, qb, qb)
    eye = jnp.eye(per, dtype=w.dtype)
    return jnp.einsum("gncd,nm->gncmd", w4, eye).reshape(nb // per, MXU_DIM, MXU_DIM)


def _fold_block_diag(w, dense):
    nb, qb, _ = w.shape
    return jnp.einsum("ncd,ndk->nck", w, dense.reshape(nb, qb, -1),
                      precision=lax.Precision.HIGHEST).reshape(nb * qb, -1)


def _rope_tables(pos, half):
    inv_freq = ROPE_BASE ** (-jnp.linspace(0.0, 1.0, half, dtype=F32))
    ang = pos.astype(F32)[:, None] * inv_freq[None, :]
    return jnp.cos(ang), jnp.sin(ang)


def _chunk(T, pref):
    return pref if T % pref == 0 else T


def _retention_layer(h2d, B, T, pos, nw, w_in, gn_w, w_out, nw_next, s0):
    D = h2d.shape[1]
    H = RET_HEADS
    dk = D // H
    dv = 2 * D // H
    M = B * T
    cos, sin = _rope_tables(pos, dk // 2)
    tm = _pick(M, (1024, 512, 256, 128))
    if T % tm:
        cos = jnp.tile(cos, (tm // T, 1))
        sin = jnp.tile(sin, (tm // T, 1))
    tn = _pick(D, (2048, 1024, 512, 256))
    qkvg = _norm_proj(h2d, nw, w_in.astype(BF16), cos, sin, tm=tm, tn=tn,
                      qk_cols=2 * D, k_cols=D, k_scale=dk ** -0.5, head_dim=dk)
    log_gamma = jnp.log1p(-jnp.exp2(-5.0 - jnp.arange(H, dtype=F32)))
    u, s_new = _retention(qkvg, log_gamma, s0, B=B, T=T, L=_chunk(T, RET_CHUNK), dk=dk, dv=dv)
    h_new, xn_next = _out_proj(u, (gn_w[:, None] * w_out).astype(BF16), h2d, nw_next,
                               tm=_pick(M, (256, 128)), final=False)
    return h_new, xn_next, s_new


def _mlstm_layer(h2d, xn, B, T, w_in, conv_w, conv_b, w_q, w_k, w_v, w_ig, b_ig, w_fg, b_fg,
                 skip, gn_w, w_out, norm_f, conv_buf, c0, n0, m0):
    D = h2d.shape[1]
    H = MLSTM_HEADS
    inner = 2 * D
    dh = inner // H
    M = B * T
    xz = _proj(xn, w_in.astype(BF16), tm=_pick(M, (1024, 512, 256, 128)),
               tn=_pick(2 * inner, (2048, 1024, 512, 256)))
    bdqk = jnp.concatenate([_block_diag_groups(w_q), _block_diag_groups(w_k) * dh ** -0.5], axis=-1).astype(BF16)
    bdv = _block_diag_groups(w_v).astype(BF16)
    wg = jnp.concatenate([w_ig, w_fg], axis=1).reshape(3, inner, 2 * H)
    wg = jnp.stack([_fold_block_diag(w_q, wg[0]) + _fold_block_diag(w_k, wg[1]), _fold_block_diag(w_v, wg[2])])
    wg = jnp.pad(wg, ((0, 0), (0, 0), (0, GATE_LANES - 2 * H))).astype(BF16)
    wg = wg.reshape(2, inner // MXU_DIM, MXU_DIM, GATE_LANES).transpose(1, 0, 2, 3).reshape(
        inner // MXU_DIM, 2 * MXU_DIM, GATE_LANES)
    bg = jnp.pad(jnp.concatenate([b_ig, b_fg]), (0, GATE_LANES - 2 * H)).reshape(1, GATE_LANES)
    hist = jnp.zeros((B, BF16_SUBLANES, inner), BF16)
    if conv_buf is not None:
        hist = hist.at[:, BF16_SUBLANES - (CONV_WIDTH - 1):].set(conv_buf.astype(BF16))
    q, k, v, xc, gates = _mlstm_pre(xz, hist, conv_w, conv_b, bdqk, bdv, wg, bg, B=B, T=T,
                                    tm=_pick(T, (256, 128)), inner=inner)
    u, c_new, n_new, m_new = _mlstm(q, k, v, gates, xc, xz, gn_w, skip, m0, c0, n0,
                                    B=B, T=T, L=_chunk(T, MLSTM_CHUNK), dh=dh)
    y = _out_proj(u, w_out.astype(BF16), h2d, norm_f, tm=_pick(M, (256, 128)), final=True)
    conv_new = xz.reshape(B, T, 2 * inner)[:, T - (CONV_WIDTH - 1):, :inner].astype(F32)
    return y, c_new, n_new, m_new, conv_new


def kernel(x_prompt, x_sample, state_ret, state_mlstm_c, state_mlstm_n, state_mlstm_m, state_mlstm_conv, norm_w, ret_w_in, ret_gn_w, ret_w_out, ml_w_in, ml_conv_w, ml_conv_b, ml_w_q, ml_w_k, ml_w_v, ml_w_ig, ml_b_ig, ml_w_fg, ml_b_fg, ml_skip, ml_gn_w, ml_w_out, norm_f):
    Bp, Tp, D = x_prompt.shape
    Bs, Ts, _ = x_sample.shape
    pos_p = jnp.arange(Tp, dtype=jnp.int32)
    pos_s = PAST_LEN + jnp.arange(Ts, dtype=jnp.int32)
    ret_w = (norm_w[0], ret_w_in[0], ret_gn_w[0], ret_w_out[0], norm_w[1])
    ml_w = (ml_w_in[0], ml_conv_w[0], ml_conv_b[0], ml_w_q[0], ml_w_k[0], ml_w_v[0],
            ml_w_ig[0], ml_b_ig[0], ml_w_fg[0], ml_b_fg[0], ml_skip[0], ml_gn_w[0], ml_w_out[0], norm_f)

    hp, xnp, ret_p = _retention_layer(x_prompt.reshape(Bp * Tp, D), Bp, Tp, pos_p, *ret_w, None)
    hs, xns, ret_s = _retention_layer(x_sample.reshape(Bs * Ts, D), Bs, Ts, pos_s, *ret_w, state_ret[0])

    m0_p = jnp.full((Bp, MLSTM_HEADS), M_INIT, F32)
    yp, mc_p, mn_p, mm_p, cv_p = _mlstm_layer(hp, xnp, Bp, Tp, *ml_w, None, None, None, m0_p)
    ys, mc_s, mn_s, mm_s, cv_s = _mlstm_layer(hs, xns, Bs, Ts, *ml_w, state_mlstm_conv[0],
                                              state_mlstm_c[0], state_mlstm_n[0], state_mlstm_m[0])
    return (yp.reshape(Bp, Tp, D), ys.reshape(Bs, Ts, D),
            ret_p[None], mc_p[None], mn_p[None], mm_p[None], cv_p[None],
            ret_s[None], mc_s[None], mn_s[None], mm_s[None], cv_s[None])
```

```python
import functools

import jax
import jax.numpy as jnp
from jax import lax
from jax.experimental import pallas as pl
from jax.experimental.pallas import tpu as pltpu

F32 = jnp.float32
BF16 = jnp.bfloat16

RET_HEADS = 8
MLSTM_HEADS = 4
CONV_WIDTH = 4
QKV_BLOCK = 4
ROPE_BASE = 10000.0
EPS = 1e-6
M_INIT = -1e30
PAST_LEN = 1024

LANES = 128
MXU_DIM = 256
BF16_SUBLANES = 16
GATE_LANES = 128
VMEM_LIMIT_BYTES = 56 * 1024 * 1024

RET_CHUNK = 256
MLSTM_CHUNK = 512


def _params(*sem):
    return pltpu.CompilerParams(dimension_semantics=sem, vmem_limit_bytes=VMEM_LIMIT_BYTES)


def _pick(n, prefs):
    for p in prefs:
        if n % p == 0:
            return p
    return n


def _silu(x):
    return x * (1.0 / (1.0 + jnp.exp(-x)))


def _norm_proj_body(x_ref, nw_ref, w_ref, cos_ref, sin_ref, o_ref, xn_ref, *,
                    rot_tiles, k_tiles, k_scale, head_dim):
    j = pl.program_id(1)

    @pl.when(j == 0)
    def _():
        x = x_ref[...]
        ms = jnp.mean(x * x, axis=-1, keepdims=True)
        xn_ref[...] = (x * lax.rsqrt(ms + EPS) * nw_ref[...]).astype(BF16)

    half = head_dim // 2

    def head_acc(hh):
        return jnp.dot(xn_ref[...], w_ref[:, hh * head_dim:(hh + 1) * head_dim], preferred_element_type=F32)

    @pl.when(j < rot_tiles)
    def _():
        scale = jnp.where(j >= rot_tiles - k_tiles, k_scale, 1.0).astype(F32)
        cos = cos_ref[...] * scale
        sin = sin_ref[...] * scale
        for hh in range(o_ref.shape[1] // head_dim):
            lo = hh * head_dim
            acc = head_acc(hh)
            x1 = acc[:, :half]
            x2 = acc[:, half:]
            o_ref[:, lo:lo + half] = (x1 * cos - x2 * sin).astype(o_ref.dtype)
            o_ref[:, lo + half:lo + head_dim] = (x1 * sin + x2 * cos).astype(o_ref.dtype)

    @pl.when(j >= rot_tiles)
    def _():
        for hh in range(o_ref.shape[1] // head_dim):
            o_ref[:, hh * head_dim:(hh + 1) * head_dim] = head_acc(hh).astype(o_ref.dtype)


def _norm_proj(x, nw, w, cos, sin, *, tm, tn, qk_cols, k_cols, k_scale, head_dim):
    M, D = x.shape
    N = w.shape[1]
    period = cos.shape[0] // tm
    half = head_dim // 2
    tab = pl.BlockSpec((tm, half), lambda i, j: (i % period, 0))
    return pl.pallas_call(
        functools.partial(_norm_proj_body, rot_tiles=qk_cols // tn, k_tiles=k_cols // tn,
                          k_scale=k_scale, head_dim=head_dim),
        grid=(M // tm, N // tn),
        in_specs=[
            pl.BlockSpec((tm, D), lambda i, j: (i, 0)),
            pl.BlockSpec((1, D), lambda i, j: (0, 0)),
            pl.BlockSpec((D, tn), lambda i, j: (0, j)),
            tab, tab,
        ],
        out_specs=pl.BlockSpec((tm, tn), lambda i, j: (i, j)),
        out_shape=jax.ShapeDtypeStruct((M, N), BF16),
        scratch_shapes=[pltpu.VMEM((tm, D), BF16)],
        compiler_params=_params("parallel", "arbitrary"),
        name="norm_proj_rot",
    )(x, nw.reshape(1, D), w, cos, sin)


def _proj_body(x_ref, w_ref, o_ref):
    o_ref[...] = jnp.dot(x_ref[...], w_ref[...], preferred_element_type=F32).astype(o_ref.dtype)


def _proj(x, w, *, tm, tn):
    M, D = x.shape
    N = w.shape[1]
    return pl.pallas_call(
        _proj_body,
        grid=(M // tm, N // tn),
        in_specs=[pl.BlockSpec((tm, D), lambda i, j: (i, 0)), pl.BlockSpec((D, tn), lambda i, j: (0, j))],
        out_specs=pl.BlockSpec((tm, tn), lambda i, j: (i, j)),
        out_shape=jax.ShapeDtypeStruct((M, N), BF16),
        compiler_params=_params("parallel", "arbitrary"),
        name="proj",
    )(x, w)


def _out_proj_body(u_ref, w_ref, h_ref, nw_ref, o_ref):
    y = h_ref[...] + jnp.dot(u_ref[...], w_ref[...], preferred_element_type=F32)
    ms = jnp.mean(y * y, axis=-1, keepdims=True)
    o_ref[...] = y * lax.rsqrt(ms + EPS) * nw_ref[...]


def _out_proj(u, w, h, nw, *, tm):
    M, K = u.shape
    N = w.shape[1]
    row = pl.BlockSpec((tm, N), lambda i: (i, 0))
    return pl.pallas_call(
        _out_proj_body,
        grid=(M // tm,),
        in_specs=[
            pl.BlockSpec((tm, K), lambda i: (i, 0)),
            pl.BlockSpec((K, N), lambda i: (0, 0)),
            row,
            pl.BlockSpec((1, N), lambda i: (0, 0)),
        ],
        out_specs=row,
        out_shape=jax.ShapeDtypeStruct((M, N), F32),
        compiler_params=_params("parallel"),
        name="out_proj_final",
    )(u, w, h, nw.reshape(1, N))


def _retention_body(*refs, L, H, dk, dv, nC, zero_init):
    if zero_init:
        (lg_ref, q_ref, k_ref, v_ref, g_ref, w_ref, h_ref, nw_ref,
         y_ref, xn_ref, s_ref, decay_ref, u_ref) = refs
    else:
        (lg_ref, q_ref, k_ref, v_ref, g_ref, w_ref, h_ref, nw_ref, s0_ref,
         y_ref, xn_ref, s_ref, decay_ref, u_ref) = refs
    c = pl.program_id(1)
    D = y_ref.shape[1]
    nc = D // H
    grp = 4 if H % 4 == 0 else (2 if H % 2 == 0 else 1)
    idx = lax.broadcasted_iota(jnp.int32, (L, 1), 0).astype(F32)
    valid = c < nC
    cur = lax.rem(c, 2)
    u_cur = u_ref.at[cur]
    u_prev = u_ref.at[1 - cur]

    @pl.when(c == 0)
    def _():
        if zero_init:
            s_ref[...] = jnp.zeros_like(s_ref)
        else:
            s_ref[...] = s0_ref[...]
        ii = lax.broadcasted_iota(jnp.int32, (L, L), 0)
        jj = lax.broadcasted_iota(jnp.int32, (L, L), 1)
        rel = jnp.maximum(ii - jj, 0).astype(F32)
        for h in range(H):
            decay_ref[h] = jnp.where(ii >= jj, jnp.exp(lg_ref[h] * rel), 0.0)
        u_ref[1] = jnp.zeros(u_ref.shape[1:], u_ref.dtype)

    ssq = jnp.zeros((L, 1), F32)
    for h in range(H):
        lg = jnp.full((1, 1), lg_ref[h], F32)
        q = q_ref[:, h * dk:(h + 1) * dk]
        k = k_ref[:, h * dk:(h + 1) * dk]
        v = v_ref[:, h * dv:(h + 1) * dv]
        scores = lax.dot_general(q, k, (((1,), (1,)), ((), ())), preferred_element_type=F32) * decay_ref[h]
        intra = jnp.dot(scores.astype(BF16), v, preferred_element_type=F32)
        s = s_ref[h]
        cross = jnp.dot(q, s.astype(BF16), preferred_element_type=F32) * jnp.exp(lg * (idx + 1.0))
        o = intra + cross
        tail = jnp.where(valid, jnp.exp(lg * (L - 1.0 - idx)), 0.0)
        carry = jnp.where(valid, jnp.exp(lg * float(L)), 1.0)
        k_tail = (k.astype(F32) * tail).astype(BF16)
        s_ref[h] = carry * s + lax.dot_general(
            k_tail, v, (((0,), (0,)), ((), ())), preferred_element_type=F32)
        o = o * lax.rsqrt(jnp.mean(o * o, axis=-1, keepdims=True) + EPS)
        g = g_ref[:, h * dv:(h + 1) * dv].astype(F32)
        u_cur[:, h * dv:(h + 1) * dv] = (_silu(g) * o).astype(BF16)

        if (h + 1) % grp == 0:
            cols = slice((h + 1 - grp) * nc, (h + 1) * nc)
            y = h_ref[:, cols] + jnp.dot(u_prev[...], w_ref[:, cols], preferred_element_type=F32)
            y_ref[:, cols] = y
            ssq = ssq + jnp.sum(y * y, axis=-1, keepdims=True)

    r = lax.rsqrt(ssq * (1.0 / D) + EPS)
    xn_ref[...] = (y_ref[...] * r * nw_ref[...]).astype(BF16)


def _retention(qkvg, log_gamma, s0, w_out, h, nw_next, *, B, T, L, dk, dv):
    H = RET_HEADS
    nC = T // L
    D = h.shape[1]
    qk_w, v_w = H * dk, H * dv
    cur_row = lambda b, c: b * nC + jnp.minimum(c, nC - 1)
    prev_row = lambda b, c: b * nC + jnp.maximum(c - 1, 0)
    out_row = pl.BlockSpec((L, D), lambda b, c: (prev_row(b, c), 0))
    in_specs = [
        pl.BlockSpec(memory_space=pltpu.SMEM),
        pl.BlockSpec((L, qk_w), lambda b, c: (cur_row(b, c), 0)),
        pl.BlockSpec((L, qk_w), lambda b, c: (cur_row(b, c), 1)),
        pl.BlockSpec((L, v_w), lambda b, c: (cur_row(b, c), 2 * qk_w // v_w)),
        pl.BlockSpec((L, v_w), lambda b, c: (cur_row(b, c), 2 * qk_w // v_w + 1)),
        pl.BlockSpec((v_w, D), lambda b, c: (0, 0), pipeline_mode=pl.Buffered(1)),
        out_row,
        pl.BlockSpec((1, D), lambda b, c: (0, 0)),
    ]
    args = [log_gamma, qkvg, qkvg, qkvg, qkvg, w_out, h, nw_next.reshape(1, D)]
    s_spec = pl.BlockSpec((None, H, dk, dv), lambda b, c: (b, 0, 0, 0))
    if s0 is not None:
        in_specs.append(s_spec)
        args.append(s0)
    return pl.pallas_call(
        functools.partial(_retention_body, L=L, H=H, dk=dk, dv=dv, nC=nC, zero_init=s0 is None),
        grid=(B, nC + 1),
        in_specs=in_specs,
        out_specs=[out_row, out_row, s_spec],
        out_shape=[
            jax.ShapeDtypeStruct((B * T, D), F32),
            jax.ShapeDtypeStruct((B * T, D), BF16),
            jax.ShapeDtypeStruct((B, H, dk, dv), F32),
        ],
        scratch_shapes=[pltpu.VMEM((H, L, L), F32), pltpu.VMEM((2, L, v_w), BF16)],
        compiler_params=_params("parallel", "arbitrary"),
        name="retention",
    )(*args)


def _mlstm_pre_body(x_ref, prev_ref, hist_ref, cw_ref, cb_ref, bdqk_ref, bdv_ref, wg_ref, bg_ref,
                    q_ref, k_ref, v_ref, xc_ref, gate_ref, *, n_gate):
    t = pl.program_id(1)
    tm, inner = x_ref.shape
    G = MXU_DIM
    first = t == 0
    row8 = lax.broadcasted_iota(jnp.int32, (8, G), 0)
    gates = bg_ref[...]
    for g in range(inner // G):
        cols = slice(g * G, (g + 1) * G)
        xb = x_ref[:, cols]
        x = xb.astype(F32)
        p = jnp.where(first, hist_ref[:, cols], prev_ref[:, cols]).astype(F32)[BF16_SUBLANES - 8:]
        conv = cb_ref[:, cols] + cw_ref[CONV_WIDTH - 1:CONV_WIDTH, cols] * x
        for s in range(1, CONV_WIDTH):
            xs = pltpu.roll(x, s, 0)
            ps = pltpu.roll(p, s, 0)
            top = jnp.where(row8 < s, ps, xs[:8])
            xs = jnp.concatenate([top, xs[8:]], axis=0) if tm > 8 else top
            conv = conv + cw_ref[CONV_WIDTH - 1 - s:CONV_WIDTH - s, cols] * xs
        xc = _silu(conv).astype(BF16)
        xc_ref[:, cols] = xc
        qk = jnp.dot(xc, bdqk_ref[g], preferred_element_type=F32)
        q_ref[:, cols] = qk[:, :G].astype(BF16)
        k_ref[:, cols] = qk[:, G:].astype(BF16)
        v_ref[:, cols] = jnp.dot(xb, bdv_ref[g], preferred_element_type=F32).astype(BF16)
        gates = gates + jnp.dot(jnp.concatenate([xc, xb], axis=1), wg_ref[g], preferred_element_type=F32)
    lane = lax.broadcasted_iota(jnp.int32, gates.shape, 1)
    log_sig = jnp.minimum(gates, 0.0) - jnp.log1p(jnp.exp(-jnp.abs(gates)))
    gate_ref[...] = jnp.where(lane >= n_gate, log_sig, gates)


def _mlstm_pre(xz, hist, cw, cb, bdqk, bdv, wg, bg, *, B, T, tm, inner):
    nT = T // tm
    pb = BF16_SUBLANES
    row_spec = pl.BlockSpec((tm, inner), lambda b, t: (b * nT + t, 0))
    in_specs = [
        row_spec,
        pl.BlockSpec((pb, inner), lambda b, t: (jnp.maximum((b * nT + t) * (tm // pb) - 1, 0), 0)),
        pl.BlockSpec((None, pb, inner), lambda b, t: (b, 0, 0)),
        pl.BlockSpec((CONV_WIDTH, inner), lambda b, t: (0, 0)),
        pl.BlockSpec((1, inner), lambda b, t: (0, 0)),
        pl.BlockSpec(bdqk.shape, lambda b, t: (0, 0, 0)),
        pl.BlockSpec(bdv.shape, lambda b, t: (0, 0, 0)),
        pl.BlockSpec(wg.shape, lambda b, t: (0, 0, 0)),
        pl.BlockSpec((1, GATE_LANES), lambda b, t: (0, 0)),
    ]
    act = jax.ShapeDtypeStruct((B * T, inner), BF16)
    return pl.pallas_call(
        functools.partial(_mlstm_pre_body, n_gate=MLSTM_HEADS),
        grid=(B, nT),
        in_specs=in_specs,
        out_specs=[row_spec, row_spec, row_spec, row_spec,
                   pl.BlockSpec((tm, GATE_LANES), lambda b, t: (b * nT + t, 0))],
        out_shape=[act, act, act, act, jax.ShapeDtypeStruct((B * T, GATE_LANES), F32)],
        compiler_params=_params("parallel", "arbitrary"),
        name="mlstm_pre",
    )(xz, xz, hist, cw, cb.reshape(1, inner), bdqk, bdv, wg, bg)


def _mlstm_body(*refs, L, zero_init):
    if zero_init:
        (m0_ref, q_ref, k_ref, v_ref, gate_ref, xc_ref, z_ref, gnw_ref, skip_ref,
         u_ref, c_ref, n_ref, m_ref) = refs
    else:
        (m0_ref, q_ref, k_ref, v_ref, gate_ref, xc_ref, z_ref, gnw_ref, skip_ref, c0_ref, n0_ref,
         u_ref, c_ref, n_ref, m_ref) = refs
    b = pl.program_id(0)
    h = pl.program_id(1)
    c = pl.program_id(2)
    H = MLSTM_HEADS

    @pl.when(c == 0)
    def _():
        if zero_init:
            c_ref[...] = jnp.zeros_like(c_ref)
            n_ref[...] = jnp.zeros_like(n_ref)
        else:
            c_ref[...] = c0_ref[...]
            n_ref[...] = n0_ref[...]
        m_ref[...] = jnp.full(m_ref.shape, m0_ref[b, h], F32)

    q = q_ref[...]
    k = k_ref[...]
    v = v_ref[...]
    gates = gate_ref[...]
    lane = lax.broadcasted_iota(jnp.int32, gates.shape, 1)
    ig_col = jnp.sum(jnp.where(lane == h, gates, 0.0), axis=1, keepdims=True)
    lf_col = jnp.sum(jnp.where(lane == H + h, gates, 0.0), axis=1, keepdims=True)
    ii = lax.broadcasted_iota(jnp.int32, (L, L), 0)
    jj = lax.broadcasted_iota(jnp.int32, (L, L), 1)
    eye = ii == jj
    causal = ii >= jj
    ig_row = jnp.sum(jnp.where(eye, ig_col, 0.0), axis=0, keepdims=True)
    lf_row = jnp.sum(jnp.where(eye, lf_col, 0.0), axis=0, keepdims=True)
    b_col = jnp.sum(jnp.where(causal, lf_row, 0.0), axis=1, keepdims=True)
    b_row = jnp.sum(jnp.where(ii <= jj, lf_col, 0.0), axis=0, keepdims=True)
    m_prev = m_ref[0:1, 0:1]
    log_d = jnp.where(causal, b_col - b_row + ig_row, -jnp.inf)
    log_past = b_col + m_prev
    m_col = jnp.maximum(log_past, jnp.max(log_d, axis=1, keepdims=True))
    d = jnp.exp(log_d - m_col)
    w_past = jnp.exp(log_past - m_col)
    s = lax.dot_general(q, k, (((1,), (1,)), ((), ())), preferred_element_type=F32) * d
    cmat = c_ref[...]
    n_prev = n_ref[...]
    num = (jnp.dot(s.astype(BF16), v, preferred_element_type=F32)
           + jnp.dot(q, cmat.astype(BF16), preferred_element_type=F32) * w_past)
    n_b = jnp.broadcast_to(n_prev, (BF16_SUBLANES, n_prev.shape[1])).astype(BF16)
    qn = lax.dot_general(q, n_b, (((1,), (1,)), ((), ())), preferred_element_type=F32)[:, 0:1]
    den = jnp.sum(s, axis=1, keepdims=True) + qn * w_past
    den = jnp.maximum(jnp.abs(den), jnp.exp(-m_col))
    hh = num * (1.0 / den)

    m_new = m_col[L - 1:L, :]
    b_last = b_col[L - 1:L, :]
    w_tail_row = jnp.exp(b_last - b_row + ig_row - m_new)
    w_tail_col = jnp.sum(jnp.where(eye, w_tail_row, 0.0), axis=1, keepdims=True)
    w_carry = jnp.exp(b_last + m_prev - m_new)
    k_tail = k.astype(F32) * w_tail_col
    c_ref[...] = w_carry * cmat + lax.dot_general(
        k_tail.astype(BF16), v, (((0,), (0,)), ((), ())), preferred_element_type=F32)
    n_ref[...] = w_carry * n_prev + jnp.sum(k_tail, axis=0, keepdims=True)
    m_ref[...] = jnp.broadcast_to(m_new, m_ref.shape)

    hh = hh - jnp.mean(hh, axis=-1, keepdims=True)
    hh = hh * lax.rsqrt(jnp.mean(hh * hh, axis=-1, keepdims=True) + EPS) * gnw_ref[...]
    o = hh + skip_ref[...] * xc_ref[...].astype(F32)
    u_ref[...] = (o * _silu(z_ref[...].astype(F32))).astype(BF16)


def _mlstm(q, k, v, gates, xc, xz, gn_w, skip, m0, c0, n0, *, B, T, L, dh):
    H = MLSTM_HEADS
    nC = T // L
    inner = H * dh
    blk = pl.BlockSpec((L, dh), lambda b, h, c: (b * nC + c, h))
    vec = pl.BlockSpec((1, dh), lambda b, h, c: (0, h))
    c_spec = pl.BlockSpec((None, None, dh, dh), lambda b, h, c: (b, h, 0, 0))
    n_spec = pl.BlockSpec((None, None, 1, dh), lambda b, h, c: (b, h, 0, 0))
    m_spec = pl.BlockSpec((None, None, 8, LANES), lambda b, h, c: (b, h, 0, 0))
    in_specs = [
        pl.BlockSpec(memory_space=pltpu.SMEM),
        blk, blk, blk,
        pl.BlockSpec((L, GATE_LANES), lambda b, h, c: (b * nC + c, 0)),
        blk,
        pl.BlockSpec((L, dh), lambda b, h, c: (b * nC + c, H + h)),
        vec, vec,
    ]
    args = [m0, q, k, v, gates, xc, xz, gn_w.reshape(1, inner), skip.reshape(1, inner)]
    if c0 is not None:
        in_specs += [c_spec, n_spec]
        args += [c0, n0.reshape(B, H, 1, dh)]
    u, c_new, n_new, m_new = pl.pallas_call(
        functools.partial(_mlstm_body, L=L, zero_init=c0 is None),
        grid=(B, H, nC),
        in_specs=in_specs,
        out_specs=[blk, c_spec, n_spec, m_spec],
        out_shape=[
            jax.ShapeDtypeStruct((B * T, inner), BF16),
            jax.ShapeDtypeStruct((B, H, dh, dh), F32),
            jax.ShapeDtypeStruct((B, H, 1, dh), F32),
            jax.ShapeDtypeStruct((B, H, 8, LANES), F32),
        ],
        compiler_params=_params("parallel", "parallel", "arbitrary"),
        name="mlstm",
    )(*args)
    return u, c_new, n_new.reshape(B, H, dh), m_new[:, :, 0, 0]


def _block_diag_groups(w):
    nb, qb, _ = w.shape
    per = MXU_DIM // qb
    w4 = w.reshape(nb // per, per, qb, qb)
    eye = jnp.eye(per, dtype=w.dtype)
    return jnp.einsum("gncd,nm->gncmd", w4, eye).reshape(nb // per, MXU_DIM, MXU_DIM)


def _fold_block_diag(w, dense):
    nb, qb, _ = w.shape
    return jnp.einsum("ncd,ndk->nck", w, dense.reshape(nb, qb, -1),
                      precision=lax.Precision.HIGHEST).reshape(nb * qb, -1)


def _rope_tables(pos, half):
    inv_freq = ROPE_BASE ** (-jnp.linspace(0.0, 1.0, half, dtype=F32))
    ang = pos.astype(F32)[:, None] * inv_freq[None, :]
    return jnp.cos(ang), jnp.sin(ang)


def _chunk(T, pref):
    return pref if T % pref == 0 else T


def _retention_layer(h2d, B, T, pos, nw, w_in, gn_w, w_out, nw_next, s0):
    D = h2d.shape[1]
    H = RET_HEADS
    dk = D // H
    dv = 2 * D // H
    M = B * T
    cos, sin = _rope_tables(pos, dk // 2)
    tm = _pick(M, (1024, 512, 256, 128))
    if T % tm:
        cos = jnp.tile(cos, (tm // T, 1))
        sin = jnp.tile(sin, (tm // T, 1))
    tn = _pick(D, (2048, 1024, 512, 256))
    qkvg = _norm_proj(h2d, nw, w_in.astype(BF16), cos, sin, tm=tm, tn=tn,
                      qk_cols=2 * D, k_cols=D, k_scale=dk ** -0.5, head_dim=dk)
    log_gamma = jnp.log1p(-jnp.exp2(-5.0 - jnp.arange(H, dtype=F32)))
    h_new, xn_next, s_new = _retention(qkvg, log_gamma, s0, (gn_w[:, None] * w_out).astype(BF16), h2d, nw_next,
                                       B=B, T=T, L=_chunk(T, RET_CHUNK), dk=dk, dv=dv)
    return h_new, xn_next, s_new


def _mlstm_layer(h2d, xn, B, T, w_in, conv_w, conv_b, w_q, w_k, w_v, w_ig, b_ig, w_fg, b_fg,
                 skip, gn_w, w_out, norm_f, conv_buf, c0, n0, m0):
    D = h2d.shape[1]
    H = MLSTM_HEADS
    inner = 2 * D
    dh = inner // H
    M = B * T
    xz = _proj(xn, w_in.astype(BF16), tm=_pick(M, (1024, 512, 256, 128)),
               tn=_pick(2 * inner, (2048, 1024, 512, 256)))
    bdqk = jnp.concatenate([_block_diag_groups(w_q), _block_diag_groups(w_k) * dh ** -0.5], axis=-1).astype(BF16)
    bdv = _block_diag_groups(w_v).astype(BF16)
    wg = jnp.concatenate([w_ig, w_fg], axis=1).reshape(3, inner, 2 * H)
    wg = jnp.stack([_fold_block_diag(w_q, wg[0]) + _fold_block_diag(w_k, wg[1]), _fold_block_diag(w_v, wg[2])])
    wg = jnp.pad(wg, ((0, 0), (0, 0), (0, GATE_LANES - 2 * H))).astype(BF16)
    wg = wg.reshape(2, inner // MXU_DIM, MXU_DIM, GATE_LANES).transpose(1, 0, 2, 3).reshape(
        inner // MXU_DIM, 2 * MXU_DIM, GATE_LANES)
    bg = jnp.pad(jnp.concatenate([b_ig, b_fg]), (0, GATE_LANES - 2 * H)).reshape(1, GATE_LANES)
    hist = jnp.zeros((B, BF16_SUBLANES, inner), BF16)
    if conv_buf is not None:
        hist = hist.at[:, BF16_SUBLANES - (CONV_WIDTH - 1):].set(conv_buf.astype(BF16))
    q, k, v, xc, gates = _mlstm_pre(xz, hist, conv_w, conv_b, bdqk, bdv, wg, bg, B=B, T=T,
                                    tm=_pick(T, (256, 128)), inner=inner)
    u, c_new, n_new, m_new = _mlstm(q, k, v, gates, xc, xz, gn_w, skip, m0, c0, n0,
                                    B=B, T=T, L=_chunk(T, MLSTM_CHUNK), dh=dh)
    y = _out_proj(u, w_out.astype(BF16), h2d, norm_f, tm=_pick(M, (256, 128)))
    conv_new = xz.reshape(B, T, 2 * inner)[:, T - (CONV_WIDTH - 1):, :inner].astype(F32)
    return y, c_new, n_new, m_new, conv_new


def kernel(x_prompt, x_sample, state_ret, state_mlstm_c, state_mlstm_n, state_mlstm_m, state_mlstm_conv, norm_w, ret_w_in, ret_gn_w, ret_w_out, ml_w_in, ml_conv_w, ml_conv_b, ml_w_q, ml_w_k, ml_w_v, ml_w_ig, ml_b_ig, ml_w_fg, ml_b_fg, ml_skip, ml_gn_w, ml_w_out, norm_f):
    Bp, Tp, D = x_prompt.shape
    Bs, Ts, _ = x_sample.shape
    pos_p = jnp.arange(Tp, dtype=jnp.int32)
    pos_s = PAST_LEN + jnp.arange(Ts, dtype=jnp.int32)
    ret_w = (norm_w[0], ret_w_in[0], ret_gn_w[0], ret_w_out[0], norm_w[1])
    ml_w = (ml_w_in[0], ml_conv_w[0], ml_conv_b[0], ml_w_q[0], ml_w_k[0], ml_w_v[0],
            ml_w_ig[0], ml_b_ig[0], ml_w_fg[0], ml_b_fg[0], ml_skip[0], ml_gn_w[0], ml_w_out[0], norm_f)

    hp, xnp, ret_p = _retention_layer(x_prompt.reshape(Bp * Tp, D), Bp, Tp, pos_p, *ret_w, None)
    hs, xns, ret_s = _retention_layer(x_sample.reshape(Bs * Ts, D), Bs, Ts, pos_s, *ret_w, state_ret[0])

    m0_p = jnp.full((Bp, MLSTM_HEADS), M_INIT, F32)
    yp, mc_p, mn_p, mm_p, cv_p = _mlstm_layer(hp, xnp, Bp, Tp, *ml_w, None, None, None, m0_p)
    ys, mc_s, mn_s, mm_s, cv_s = _mlstm_layer(hs, xns, Bs, Ts, *ml_w, state_mlstm_conv[0],
                                              state_mlstm_c[0], state_mlstm_n[0], state_mlstm_m[0])
    return (yp.reshape(Bp, Tp, D), ys.reshape(Bs, Ts, D),
            ret_p[None], mc_p[None], mn_p[None], mm_p[None], cv_p[None],
            ret_s[None], mc_s[None], mn_s[None], mm_s[None], cv_s[None])
```
